```python
import math
import jax, jax.numpy as jnp
from jax import lax
import numpy as np


D_MODEL = 1024
BATCH = 8
SEQ = 2048
DEPTH = 4

CHUNK = 64
N_META = 16
HEAD_DIM = 64
D_CONV = 256
CONV_WIDTH = 31
RWKV_HEADS = 6
D_RWKV = RWKV_HEADS * HEAD_DIM
DECAY_LORA = 64
AAA_LORA = 64
GATE_LORA = 128
SB_HEADS = 6
D_SB = SB_HEADS * HEAD_DIM
SB_BLOCK = 128
D_MIX = D_CONV + D_RWKV + D_SB
D_RWKV_IN = 3 * D_RWKV + DECAY_LORA + AAA_LORA + GATE_LORA
D_IN = 2 * D_CONV + D_RWKV_IN + 3 * D_SB
D_FF = 2816
RMS_EPS = 1e-6
LN_EPS = 1e-5
GN_EPS = 64e-5

kernel_name = 'hymba_conformer_rwkv7_stickbreak_macaron'


def rms_norm(x, g):
    xf = x.astype(jnp.float32)
    y = xf * lax.rsqrt(jnp.mean(xf * xf, axis=-1, keepdims=True) + RMS_EPS)
    return (y * g.astype(jnp.float32)).astype(x.dtype)


def swiglu_ffn(h, w13, w2):
    gate, up = jnp.split(h @ w13, 2, axis=-1)
    return (jax.nn.silu(gate) * up) @ w2


def conv_group(p_val, p_gate, conv_w, conv_b, ln_g, ln_b):
    c = p_val * jax.nn.sigmoid(p_gate)
    y = lax.conv_general_dilated(
        c, conv_w[:, None, :], window_strides=(1,), padding=[(CONV_WIDTH - 1, 0)],
        dimension_numbers=('NWC', 'WIO', 'NWC'), feature_group_count=D_CONV) + conv_b
    yf = y.astype(jnp.float32)
    mu = jnp.mean(yf, axis=-1, keepdims=True)
    var = jnp.mean(jnp.square(yf - mu), axis=-1, keepdims=True)
    yn = (yf - mu) * lax.rsqrt(var + LN_EPS) * ln_g.astype(jnp.float32) + ln_b.astype(jnp.float32)
    return jax.nn.silu(yn).astype(p_val.dtype)


def rwkv7_group(p, mu, w0, wB, a0, aB, gB, k_k, k_a, r_k, ln_g, ln_b):
    f32 = jnp.float32
    out_dtype = p.dtype
    p = p.astype(f32)
    B, L, _ = p.shape
    prev = jnp.pad(p[:, :-1], ((0, 0), (1, 0), (0, 0)))
    xs = p + mu.astype(f32) * (prev - p)
    r, k, v, wd, ad, gd = jnp.split(
        xs, [D_RWKV, 2 * D_RWKV, 3 * D_RWKV, 3 * D_RWKV + DECAY_LORA,
             3 * D_RWKV + DECAY_LORA + AAA_LORA], axis=-1)
    w_log = -jax.nn.softplus(-(w0.astype(f32) + jnp.tanh(wd) @ wB.astype(f32))) - 0.5
    decay = jnp.exp(-jnp.exp(w_log))
    a = jax.nn.sigmoid(a0.astype(f32) + ad @ aB.astype(f32))
    g = jax.nn.sigmoid(gd) @ gB.astype(f32)

    def heads(t):
        return t.reshape(B, L, RWKV_HEADS, HEAD_DIM)

    kk = heads(k * k_k.astype(f32))
    kk = kk / jnp.maximum(jnp.sqrt(jnp.sum(kk * kk, axis=-1, keepdims=True)), 1e-12)
    k = k * (1.0 + (a - 1.0) * k_a.astype(f32))
    rh, kh, vh, wh, ah = heads(r), heads(k), heads(v), heads(decay), heads(a)

    def step(S, inp):
        r_t, w_t, k_t, v_t, na_t, b_t = inp
        sa = jnp.einsum('bhvk,bhk->bhv', S, na_t)
        S = S * w_t[:, :, None, :] + sa[..., None] * b_t[:, :, None, :] + v_t[..., None] * k_t[:, :, None, :]
        return S, jnp.einsum('bhvk,bhk->bhv', S, r_t)

    seq_in = tuple(t.transpose(1, 0, 2, 3) for t in (rh, wh, kh, vh, -kk, kk * ah))
    S0 = jnp.zeros((B, RWKV_HEADS, HEAD_DIM, HEAD_DIM), f32)
    _, o = lax.scan(step, S0, seq_in)
    o = o.transpose(1, 0, 2, 3)
    m = jnp.mean(o, axis=-1, keepdims=True)
    var = jnp.mean(jnp.square(o - m), axis=-1, keepdims=True)
    o = ((o - m) * lax.rsqrt(var + GN_EPS)).reshape(B, L, D_RWKV)
    o = o * ln_g.astype(f32) + ln_b.astype(f32)
    bonus = jnp.sum(rh * kh * r_k.astype(f32), axis=-1, keepdims=True) * vh
    o = (o + bonus.reshape(B, L, D_RWKV)) * g
    return o.astype(out_dtype)


def stick_breaking_group(q, k, v, norm_g):
    f32 = jnp.float32
    B, L, _ = q.shape
    n_blk = -(-L // SB_BLOCK)
    Lp = n_blk * SB_BLOCK

    def heads(t):
        t = jnp.pad(t, ((0, 0), (0, Lp - L), (0, 0)))
        return t.reshape(B, Lp, SB_HEADS, HEAD_DIM).transpose(0, 2, 1, 3)

    qh, kh, vh = heads(q), heads(k), heads(v)
    scale = HEAD_DIM ** -0.5
    outs = []
    for i in range(n_blk):
        q0 = i * SB_BLOCK
        kend = q0 + SB_BLOCK
        qb = qh[:, :, q0:kend].astype(f32)
        kb = kh[:, :, :kend].astype(f32)
        z = jnp.einsum('bhqd,bhkd->bhqk', qb, kb) * scale
        t_idx = q0 + jnp.arange(SB_BLOCK)
        s_idx = jnp.arange(kend)
        causal = s_idx[None, :] < t_idx[:, None]
        log_rest = jnp.where(causal, jax.nn.log_sigmoid(-z), 0.0)
        after = lax.cumsum(log_rest, axis=3, reverse=True) - log_rest
        log_a = jnp.where(causal, jax.nn.log_sigmoid(z) + after, -jnp.inf)
        outs.append(jnp.einsum('bhqk,bhkd->bhqd', jnp.exp(log_a), vh[:, :, :kend].astype(f32)))
    o = jnp.concatenate(outs, axis=2)[:, :, :L].transpose(0, 2, 1, 3)
    o = o * lax.rsqrt(jnp.mean(o * o, axis=-1, keepdims=True) + RMS_EPS)
    o = o * norm_g.astype(f32).reshape(SB_HEADS, HEAD_DIM)
    return o.reshape(B, L, D_SB).astype(q.dtype)


def setup_inputs(seed: int = 0) -> dict:
    key = jax.random.key(seed)
    ks = iter(jax.random.split(key, 40))
    f32 = jnp.float32

    def nrm(shape, scale):
        return scale * jax.random.normal(next(ks), shape, f32)

    def gain(shape):
        return 1.0 + 0.02 * jax.random.normal(next(ks), shape, f32)

    def unif(shape, lo, hi):
        return jax.random.uniform(next(ks), shape, f32, lo, hi)

    return {
        'x': nrm((BATCH, SEQ, D_MODEL), 1.0),
        'meta': nrm((N_META, D_MODEL), 1.0),
        'ffn1_norm': gain((DEPTH, D_MODEL)),
        'ffn1_w13': nrm((DEPTH, D_MODEL, 2 * D_FF), D_MODEL ** -0.5),
        'ffn1_w2': nrm((DEPTH, D_FF, D_MODEL), D_FF ** -0.5),
        'mix_norm': gain((DEPTH, D_MODEL)),
        'w_in': nrm((DEPTH, D_MODEL, D_IN), D_MODEL ** -0.5),
        'conv_w': nrm((DEPTH, CONV_WIDTH, D_CONV), CONV_WIDTH ** -0.5),
        'conv_b': nrm((DEPTH, D_CONV), 0.02),
        'conv_ln_g': gain((DEPTH, D_CONV)),
        'conv_ln_b': nrm((DEPTH, D_CONV), 0.02),
        'rwkv_mu': unif((DEPTH, D_RWKV_IN), 0.0, 1.0),
        'rwkv_w0': unif((DEPTH, D_RWKV), -5.0, 1.0),
        'rwkv_wB': nrm((DEPTH, DECAY_LORA, D_RWKV), 0.1),
        'rwkv_a0': nrm((DEPTH, D_RWKV), 0.5),
        'rwkv_aB': nrm((DEPTH, AAA_LORA, D_RWKV), AAA_LORA ** -0.5),
        'rwkv_gB': nrm((DEPTH, GATE_LORA, D_RWKV), GATE_LORA ** -0.5),
        'rwkv_kk': 0.85 + nrm((DEPTH, D_RWKV), 0.05),
        'rwkv_ka': 1.0 + nrm((DEPTH, D_RWKV), 0.05),
        'rwkv_rk': nrm((DEPTH, RWKV_HEADS, HEAD_DIM), 0.1),
        'rwkv_ln_g': gain((DEPTH, D_RWKV)),
        'rwkv_ln_b': nrm((DEPTH, D_RWKV), 0.02),
        'sb_norm': gain((DEPTH, D_SB)),
        'w_out': nrm((DEPTH, D_MIX, D_MODEL), D_MIX ** -0.5),
        'ffn2_norm': gain((DEPTH, D_MODEL)),
        'ffn2_w13': nrm((DEPTH, D_MODEL, 2 * D_FF), D_MODEL ** -0.5),
        'ffn2_w2': nrm((DEPTH, D_FF, D_MODEL), D_FF ** -0.5),
        'final_norm': gain((D_MODEL,)),
    }


def reference(x, meta, ffn1_norm, ffn1_w13, ffn1_w2, mix_norm, w_in, conv_w, conv_b,
              conv_ln_g, conv_ln_b, rwkv_mu, rwkv_w0, rwkv_wB, rwkv_a0, rwkv_aB, rwkv_gB,
              rwkv_kk, rwkv_ka, rwkv_rk, rwkv_ln_g, rwkv_ln_b, sb_norm, w_out,
              ffn2_norm, ffn2_w13, ffn2_w2, final_norm):
    B = x.shape[0]
    meta_b = jnp.broadcast_to(meta.astype(x.dtype)[None], (B, N_META, D_MODEL))
    h = jnp.concatenate([meta_b, x], axis=1)
    split_pts = [D_CONV, 2 * D_CONV, 2 * D_CONV + D_RWKV_IN,
                 2 * D_CONV + D_RWKV_IN + D_SB, 2 * D_CONV + D_RWKV_IN + 2 * D_SB]
    for l in range(DEPTH):
        h = h + 0.5 * swiglu_ffn(rms_norm(h, ffn1_norm[l]), ffn1_w13[l], ffn1_w2[l])
        p = rms_norm(h, mix_norm[l]) @ w_in[l]
        pc_val, pc_gate, p_rwkv, q_sb, k_sb, v_sb = jnp.split(p, split_pts, axis=-1)
        y_conv = conv_group(pc_val, pc_gate, conv_w[l], conv_b[l], conv_ln_g[l], conv_ln_b[l])
        y_rwkv = rwkv7_group(p_rwkv, rwkv_mu[l], rwkv_w0[l], rwkv_wB[l], rwkv_a0[l], rwkv_aB[l],
                             rwkv_gB[l], rwkv_kk[l], rwkv_ka[l], rwkv_rk[l], rwkv_ln_g[l], rwkv_ln_b[l])
        y_sb = stick_breaking_group(q_sb, k_sb, v_sb, sb_norm[l])
        y = jnp.concatenate([y_conv, y_rwkv, y_sb], axis=-1)
        h = h + y @ w_out[l]
        h = h + 0.5 * swiglu_ffn(rms_norm(h, ffn2_norm[l]), ffn2_w13[l], ffn2_w2[l])
    h = rms_norm(h, final_norm)
    return h[:, N_META:]
```

```python
import functools

import jax
import jax.numpy as jnp
from jax import lax
from jax.experimental import pallas as pl
from jax.experimental.pallas import tpu as pltpu

F32 = jnp.float32
BF16 = jnp.bfloat16

N_META = 16
HEAD_DIM = 64
D_CONV = 256
CONV_WIDTH = 31
D_RWKV = 384
D_SB = 384
DECAY_LORA = 64
AAA_LORA = 64
GATE_LORA = 128
D_RWKV_IN = 3 * D_RWKV + DECAY_LORA + AAA_LORA + GATE_LORA
RMS_EPS = 1e-6
LN_EPS = 1e-5
GN_EPS = 64e-5

LANES = 128
SEQ_ALIGN = 128
RWKV_CHUNK = 64
VMEM_LIMIT = 56 * 1024 * 1024


def _cparams(n_axes):
    return pltpu.CompilerParams(dimension_semantics=("arbitrary",) * n_axes, vmem_limit_bytes=VMEM_LIMIT)


def _dot(a, b):
    return jnp.dot(a.astype(BF16), b.astype(BF16), preferred_element_type=F32)


def _dot_nt(a, b):
    return lax.dot_general(a.astype(BF16), b.astype(BF16), (((1,), (1,)), ((), ())), preferred_element_type=F32)


def _dot_exact_rhs(x, m):
    h1 = x.astype(BF16)
    r1 = x - h1.astype(F32)
    h2 = r1.astype(BF16)
    h3 = (r1 - h2.astype(F32)).astype(BF16)
    return (jnp.dot(h1, m, preferred_element_type=F32) + jnp.dot(h2, m, preferred_element_type=F32)
            + jnp.dot(h3, m, preferred_element_type=F32))


def _dot_exact_lhs(m, x):
    h1 = x.astype(BF16)
    r1 = x - h1.astype(F32)
    h2 = r1.astype(BF16)
    h3 = (r1 - h2.astype(F32)).astype(BF16)
    return (jnp.dot(m, h1, preferred_element_type=F32) + jnp.dot(m, h2, preferred_element_type=F32)
            + jnp.dot(m, h3, preferred_element_type=F32))


def _sigmoid(x):
    return 1.0 / (1.0 + jnp.exp(-x))


def _softplus(x):
    return jnp.maximum(x, 0.0) + jnp.log(1.0 + jnp.exp(-jnp.abs(x)))


def _rms_norm_rows(x, g):
    ms = jnp.mean(x * x, axis=-1, keepdims=True)
    return x * lax.rsqrt(ms + RMS_EPS) * g


def _ffn_kernel(h_ref, g_ref, wg_ref, wu_ref, w2_ref, o_ref, xn_ref, acc_ref, *, n_f):
    f = pl.program_id(1)

    @pl.when(f == 0)
    def _():
        xn_ref[...] = _rms_norm_rows(h_ref[...], g_ref[...]).astype(BF16)
        acc_ref[...] = jnp.zeros_like(acc_ref)

    xn = xn_ref[...]
    gate = jnp.dot(xn, wg_ref[...], preferred_element_type=F32)
    up = jnp.dot(xn, wu_ref[...], preferred_element_type=F32)
    act = (gate * _sigmoid(gate) * up).astype(BF16)
    acc_ref[...] += jnp.dot(act, w2_ref[...], preferred_element_type=F32)

    @pl.when(f == n_f - 1)
    def _():
        o_ref[...] = h_ref[...] + 0.5 * acc_ref[...]


def _ffn(h2d, norm_g, w13, w2, layer, *, tm=1024, tf=256):
    m, d = h2d.shape
    d_ff = w2.shape[1]
    assert m % tm == 0 and d_ff % tf == 0
    n_f = d_ff // tf
    return pl.pallas_call(
        functools.partial(_ffn_kernel, n_f=n_f),
        grid=(m // tm, n_f),
        in_specs=[
            pl.BlockSpec((tm, d), lambda i, f: (i, 0)),
            pl.BlockSpec((None, 1, d), lambda i, f: (layer, 0, 0)),
            pl.BlockSpec((None, d, tf), lambda i, f: (layer, 0, f)),
            pl.BlockSpec((None, d, tf), lambda i, f: (layer, 0, f + n_f)),
            pl.BlockSpec((None, tf, d), lambda i, f: (layer, f, 0)),
        ],
        out_specs=pl.BlockSpec((tm, d), lambda i, f: (i, 0)),
        out_shape=jax.ShapeDtypeStruct((m, d), F32),
        scratch_shapes=[pltpu.VMEM((tm, d), BF16), pltpu.VMEM((tm, d), F32)],
        compiler_params=_cparams(2),
        name="ffn",
    )(h2d, norm_g, w13, w13, w2)


def _inproj_kernel(h_ref, g_ref, w_ref, pc_ref, pr_ref, ps_ref):
    xn = _rms_norm_rows(h_ref[...], g_ref[...]).astype(BF16)
    c0 = 2 * D_CONV
    c1 = c0 + D_RWKV_IN
    pc_ref[...] = jnp.dot(xn, w_ref[:, 0:c0], preferred_element_type=F32)
    pr_ref[...] = jnp.dot(xn, w_ref[:, c0:c1], preferred_element_type=F32)
    ps_ref[...] = jnp.dot(xn, w_ref[:, c1:], preferred_element_type=F32)


def _inproj(h2d, norm_g, w_in, layer, *, tm=512):
    m, d = h2d.shape
    assert m % tm == 0
    d_in = w_in.shape[2]
    widths = (2 * D_CONV, D_RWKV_IN, 3 * D_SB)
    return pl.pallas_call(
        _inproj_kernel,
        grid=(m // tm,),
        in_specs=[
            pl.BlockSpec((tm, d), lambda i: (i, 0)),
            pl.BlockSpec((None, 1, d), lambda i: (layer, 0, 0)),
            pl.BlockSpec((None, d, d_in), lambda i: (layer, 0, 0)),
        ],
        out_specs=[pl.BlockSpec((tm, w), lambda i: (i, 0)) for w in widths],
        out_shape=[jax.ShapeDtypeStruct((m, w), F32) for w in widths],
        compiler_params=_cparams(1),
        name="inproj",
    )(h2d, norm_g, w_in)


CONV_PAD = 32
CONV_SUB = 64


def _conv_kernel(p_ref, w_ref, b_ref, lg_ref, lb_ref, o_ref, buf_ref, *, tt):
    t = pl.program_id(1)

    @pl.when(t == 0)
    def _():
        buf_ref[0:CONV_PAD, :] = jnp.zeros((CONV_PAD, D_CONV), F32)

    val = p_ref[0, :, 0:D_CONV]
    gate = p_ref[0, :, D_CONV:2 * D_CONV]
    buf_ref[CONV_PAD:CONV_PAD + tt, :] = val * _sigmoid(gate)
    first = CONV_PAD - (CONV_WIDTH - 1)
    for s in range(tt // CONV_SUB):
        acc = jnp.zeros((CONV_SUB, D_CONV), F32)
        for j in range(CONV_WIDTH):
            r0 = s * CONV_SUB + first + j
            acc = acc + w_ref[j:j + 1, :] * buf_ref[r0:r0 + CONV_SUB, :]
        y = acc + b_ref[...]
        mu = jnp.mean(y, axis=-1, keepdims=True)
        dlt = y - mu
        var = jnp.mean(dlt * dlt, axis=-1, keepdims=True)
        yn = dlt * lax.rsqrt(var + LN_EPS) * lg_ref[...] + lb_ref[...]
        o_ref[0, s * CONV_SUB:(s + 1) * CONV_SUB, :] = yn * _sigmoid(yn)
    buf_ref[0:CONV_PAD, :] = buf_ref[tt:tt + CONV_PAD, :]


def _conv_group(pc, conv_w, conv_b, ln_g, ln_b, layer, *, tt=128):
    b, lp, _ = pc.shape
    assert lp % tt == 0 and tt % CONV_SUB == 0
    vec = lambda: pl.BlockSpec((None, 1, D_CONV), lambda i, t: (layer, 0, 0))
    return pl.pallas_call(
        functools.partial(_conv_kernel, tt=tt),
        grid=(b, lp // tt),
        in_specs=[
            pl.BlockSpec((1, tt, 2 * D_CONV), lambda i, t: (i, t, 0)),
            pl.BlockSpec((None, CONV_WIDTH, D_CONV), lambda i, t: (layer, 0, 0)),
            vec(), vec(), vec(),
        ],
        out_specs=pl.BlockSpec((1, tt, D_CONV), lambda i, t: (i, t, 0)),
        out_shape=jax.ShapeDtypeStruct((b, lp, D_CONV), F32),
        scratch_shapes=[pltpu.VMEM((tt + CONV_PAD, D_CONV), F32)],
        compiler_params=_cparams(2),
        name="conv_group",
    )(pc, conv_w, conv_b, ln_g, ln_b)


def _rwkv_kernel(p_ref, mu_ref, w0_ref, wb_ref, a0_ref, ab_ref, gb_ref, kk_ref, ka_ref, rk_ref, lg_ref, lb_ref,
                 bd_ref, tri_ref, o_ref, st_ref, carry_ref, *, tt):
    c = RWKV_CHUNK
    c2 = 2 * c
    n_pairs = D_RWKV // LANES
    t = pl.program_id(1)

    @pl.when(t == 0)
    def _():
        st_ref[...] = jnp.zeros_like(st_ref)
        carry_ref[...] = jnp.zeros_like(carry_ref)

    x = p_ref[0]
    row = lax.broadcasted_iota(jnp.int32, x.shape, 0)
    prev = jnp.where(row == 0, carry_ref[...], pltpu.roll(x, 1, 0))
    carry_ref[...] = x[tt - 1:tt, :]
    xs = x + mu_ref[...] * (prev - x)

    r = xs[:, 0:D_RWKV]
    k = xs[:, D_RWKV:2 * D_RWKV]
    v = xs[:, 2 * D_RWKV:3 * D_RWKV]
    lora_in = xs[:, 3 * D_RWKV:3 * D_RWKV + DECAY_LORA + AAA_LORA]
    gd = xs[:, 3 * D_RWKV + DECAY_LORA + AAA_LORA:]

    bd = bd_ref[...]

    def head_sum(z):
        hi = z.astype(BF16)
        lo = (z - hi.astype(F32)).astype(BF16)
        return jnp.dot(hi, bd, preferred_element_type=F32) + jnp.dot(lo, bd, preferred_element_type=F32)

    w_log = -_softplus(-(w0_ref[...] + _dot(jnp.tanh(lora_in), wb_ref[...]))) - 0.5
    logw = -jnp.exp(w_log)
    a = _sigmoid(a0_ref[...] + _dot(lora_in, ab_ref[...]))
    g = _dot(_sigmoid(gd), gb_ref[...])
    kk = k * kk_ref[...]
    kk = kk / jnp.maximum(jnp.sqrt(head_sum(kk * kk)), 1e-12)
    kq = k * (1.0 + (a - 1.0) * ka_ref[...])
    na = -kk
    b = kk * a
    bonus = head_sum(r * kq * rk_ref[...]) * v

    lane = lax.broadcasted_iota(jnp.int32, (c, LANES), 1)
    lane2 = lax.broadcasted_iota(jnp.int32, (c2, LANES), 1)
    m0 = lane < HEAD_DIM
    m0_2 = lane2 < HEAD_DIM
    ri = lax.broadcasted_iota(jnp.int32, (c2, c2), 0)
    cj = lax.broadcasted_iota(jnp.int32, (c2, c2), 1)
    ti = jnp.bitwise_and(ri, c - 1)
    sj = jnp.bitwise_and(cj, c - 1)
    mask_z = (ti > sj) | ((ri >= c) & (ti == sj))
    left_top = lax.broadcasted_iota(jnp.int32, (c, c2), 1) < c
    blockdiag = jnp.bitwise_and(ri, c) == jnp.bitwise_and(cj, c)
    eye = (ri == cj)
    zero = jnp.zeros((c, LANES), F32)
    tri = tri_ref[...]

    out_rows = []
    states = [st_ref[j] for j in range(n_pairs)]
    for s in range(tt // c):
        rs = slice(s * c, (s + 1) * c)
        cl_all = _dot_exact_lhs(tri, logw[rs])
        out_pairs = []
        for j in range(n_pairs):
            ls = slice(j * LANES, (j + 1) * LANES)
            lw, cl = logw[rs, ls], cl_all[:, ls]
            cl_end = cl[c - 1:c, :]
            e_pos = jnp.exp(cl)
            e_neg = jnp.exp(-cl)
            e_end = jnp.exp(cl_end - cl)
            at = jnp.exp(cl - lw) * na[rs, ls]
            bt = b[rs, ls] * e_neg
            kt = kq[rs, ls] * e_neg
            rt = r[rs, ls] * e_pos
            bp = b[rs, ls] * e_end
            kp = kq[rs, ls] * e_end
            p_end = jnp.exp(cl_end)
            vv = v[rs, ls]

            xx = jnp.concatenate([at, rt], axis=0)
            z0 = jnp.where(mask_z, _dot_nt(jnp.where(m0_2, xx, 0.0), jnp.concatenate([bt, kt], axis=0)), 0.0)
            z1 = jnp.where(mask_z, _dot_nt(jnp.where(m0_2, 0.0, xx), jnp.concatenate([kt, bt], axis=0)), 0.0)
            z0t, z1t = z0[0:c], z1[0:c]
            l_bd = jnp.concatenate([jnp.where(left_top, z0t, 0.0), jnp.where(left_top, 0.0, z1t)], axis=0)
            tinv = jnp.where(eye, 1.0, 0.0) + l_bd
            pw = l_bd
            for _ in range(5):
                pw = _dot(pw, pw)
                tinv = tinv + _dot(tinv, pw)

            v2 = jnp.concatenate([vv, vv], axis=0)
            akv = jnp.where(m0, _dot(jnp.where(left_top, 0.0, z0t), v2), _dot(jnp.where(left_top, z1t, 0.0), v2))
            wa = jnp.concatenate([jnp.where(m0, at, 0.0), jnp.where(m0, 0.0, at)], axis=0)
            wv = jnp.concatenate([jnp.where(m0, akv, 0.0), jnp.where(m0, 0.0, akv)], axis=0)
            ta = _dot(tinv, wa)
            tv = _dot(tinv, wv)
            ah = ta[0:c] + ta[c:c2]
            uv = tv[0:c] + tv[c:c2]

            z0b, z1b = z0[c:c2], z1[c:c2]
            rba = jnp.where(m0, _dot(z0b, jnp.concatenate([ah, zero], axis=0)),
                            _dot(z1b, jnp.concatenate([zero, ah], axis=0)))
            ov = jnp.where(m0, _dot(z0b, jnp.concatenate([uv, vv], axis=0)),
                           _dot(z1b, jnp.concatenate([vv, uv], axis=0)))
            rh = rt + rba

            ypt = jnp.concatenate([bp, kp], axis=0).T
            gm = _dot(ypt, jnp.concatenate([ah, zero], axis=0))
            hm = _dot(ypt, jnp.concatenate([uv, vv], axis=0))
            gmat = jnp.where(blockdiag, gm, 0.0) + jnp.where(eye, jnp.broadcast_to(p_end, (c2, LANES)), 0.0)
            hmat = jnp.where(blockdiag, hm, 0.0)

            st = states[j]
            out_pairs.append(_dot(rh, st) + ov)
            states[j] = _dot(gmat, st) + hmat
        out_rows.append(jnp.concatenate(out_pairs, axis=1))
    for j in range(n_pairs):
        st_ref[j] = states[j]

    o = jnp.concatenate(out_rows, axis=0) if len(out_rows) > 1 else out_rows[0]
    mean = head_sum(o) * (1.0 / HEAD_DIM)
    dlt = o - mean
    var = head_sum(dlt * dlt) * (1.0 / HEAD_DIM)
    on = dlt * lax.rsqrt(var + GN_EPS) * lg_ref[...] + lb_ref[...]
    o_ref[0] = (on + bonus) * g


def _rwkv_group(pr, mu, w0, wb_ext, a0, ab_ext, gb, k_k, k_a, r_k, ln_g, ln_b, bd, tri, layer, *, tt=128):
    b, lp, _ = pr.shape
    assert lp % tt == 0 and tt % RWKV_CHUNK == 0
    n_pairs = D_RWKV // LANES

    def par(arr):
        return pl.BlockSpec((None,) + arr.shape[1:], lambda i, t: (layer,) + (0,) * (arr.ndim - 1))

    def const(arr):
        return pl.BlockSpec(arr.shape, lambda i, t: (0,) * arr.ndim)

    params = (mu, w0, wb_ext, a0, ab_ext, gb, k_k, k_a, r_k, ln_g, ln_b)
    return pl.pallas_call(
        functools.partial(_rwkv_kernel, tt=tt),
        grid=(b, lp // tt),
        in_specs=[pl.BlockSpec((1, tt, D_RWKV_IN), lambda i, t: (i, t, 0))] + [par(p) for p in params]
        + [const(bd), const(tri)],
        out_specs=pl.BlockSpec((1, tt, D_RWKV), lambda i, t: (i, t, 0)),
        out_shape=jax.ShapeDtypeStruct((b, lp, D_RWKV), F32),
        scratch_shapes=[pltpu.VMEM((n_pairs, LANES, LANES), F32), pltpu.VMEM((1, D_RWKV_IN), F32)],
        compiler_params=_cparams(2),
        name="rwkv_group",
    )(pr, *params, bd, tri)


SB_TILE = 128


def _sb_kernel(q_ref, k_ref, v_ref, g_ref, bd_ref, o_ref):
    tq = tk = SB_TILE
    qi = pl.program_id(2)
    q = q_ref[0] * (HEAD_DIM ** -0.5)
    lane = lax.broadcasted_iota(jnp.int32, (tq, LANES), 1)
    m0 = lane < HEAD_DIM
    q0 = jnp.where(m0, q, 0.0).astype(BF16)
    q1 = jnp.where(m0, 0.0, q).astype(BF16)
    rowi = lax.broadcasted_iota(jnp.int32, (tq, tk), 0)
    coli = lax.broadcasted_iota(jnp.int32, (tq, tk), 1)
    later = jnp.where(rowi > coli, 1.0, 0.0).astype(BF16)

    def body(it, carry):
        acc0, acc1, c0, c1 = carry
        kb = qi - it
        off = pl.multiple_of(kb * tk, tk)
        kblk = k_ref[0, pl.ds(off, tk), :].astype(BF16)
        vblk = v_ref[0, pl.ds(off, tk), :].astype(BF16)
        causal = (coli + kb * tk) < (rowi + qi * tq)

        def head(qh, acc, cc):
            z = lax.dot_general(qh, kblk, (((1,), (1,)), ((), ())), preferred_element_type=F32)
            lr = jnp.minimum(-z, 0.0) - jnp.log(1.0 + jnp.exp(-jnp.abs(z)))
            lrm = jnp.where(causal, lr, 0.0)
            hi = lrm.astype(BF16)
            lo = (lrm - hi.astype(F32)).astype(BF16)
            after = (jnp.dot(hi, later, preferred_element_type=F32) + jnp.dot(lo, later, preferred_element_type=F32)
                     + cc)
            att = jnp.where(causal, jnp.exp(z + lr + after), 0.0)
            acc = acc + jnp.dot(att.astype(BF16), vblk, preferred_element_type=F32)
            return acc, cc + jnp.sum(lrm, axis=1, keepdims=True)

        acc0, c0 = head(q0, acc0, c0)
        acc1, c1 = head(q1, acc1, c1)
        return acc0, acc1, c0, c1

    zacc = jnp.zeros((tq, LANES), F32)
    zc = jnp.zeros((tq, 1), F32)
    acc0, acc1, _, _ = lax.fori_loop(0, qi + 1, body, (zacc, zacc, zc, zc))
    o = jnp.where(m0, acc0, acc1)
    sq = o * o
    hi = sq.astype(BF16)
    lo = (sq - hi.astype(F32)).astype(BF16)
    ss = jnp.dot(hi, bd_ref[...], preferred_element_type=F32) + jnp.dot(lo, bd_ref[...], preferred_element_type=F32)
    o_ref[0] = o * lax.rsqrt(ss * (1.0 / HEAD_DIM) + RMS_EPS) * g_ref[...]


def _sb_group(ps, norm_g, bd128, layer):
    b, lp, _ = ps.shape
    assert lp % SB_TILE == 0
    n_pairs = D_SB // LANES
    n_q = lp // SB_TILE
    return pl.pallas_call(
        _sb_kernel,
        grid=(b, n_pairs, n_q),
        in_specs=[
            pl.BlockSpec((1, SB_TILE, LANES), lambda i, j, q: (i, q, j)),
            pl.BlockSpec((1, lp, LANES), lambda i, j, q: (i, 0, n_pairs + j)),
            pl.BlockSpec((1, lp, LANES), lambda i, j, q: (i, 0, 2 * n_pairs + j)),
            pl.BlockSpec((None, 1, LANES), lambda i, j, q: (layer, 0, j)),
            pl.BlockSpec((LANES, LANES), lambda i, j, q: (0, 0)),
        ],
        out_specs=pl.BlockSpec((1, SB_TILE, LANES), lambda i, j, q: (i, q, j)),
        out_shape=jax.ShapeDtypeStruct((b, lp, D_SB), F32),
        compiler_params=_cparams(3),
        name="sb_group",
    )(ps, ps, ps, norm_g, bd128)


def _outproj_kernel(h_ref, yc_ref, yr_ref, ys_ref, w_ref, o_ref):
    r0 = D_CONV
    r1 = D_CONV + D_RWKV
    acc = jnp.dot(yc_ref[...].astype(BF16), w_ref[0:r0, :], preferred_element_type=F32)
    acc = acc + jnp.dot(yr_ref[...].astype(BF16), w_ref[r0:r1, :], preferred_element_type=F32)
    acc = acc + jnp.dot(ys_ref[...].astype(BF16), w_ref[r1:, :], preferred_element_type=F32)
    o_ref[...] = h_ref[...] + acc


def _outproj(h2d, yc, yr, ys, w_out, layer, *, tm=1024):
    m, d = h2d.shape
    assert m % tm == 0
    rows = lambda w: pl.BlockSpec((tm, w), lambda i: (i, 0))
    return pl.pallas_call(
        _outproj_kernel,
        grid=(m // tm,),
        in_specs=[rows(d), rows(D_CONV), rows(D_RWKV), rows(D_SB),
                  pl.BlockSpec((None,) + w_out.shape[1:], lambda i: (layer, 0, 0))],
        out_specs=rows(d),
        out_shape=jax.ShapeDtypeStruct((m, d), F32),
        compiler_params=_cparams(1),
        name="outproj",
    )(h2d, yc, yr, ys, w_out)


def _final_norm_kernel(h_ref, g_ref, o_ref):
    o_ref[...] = _rms_norm_rows(h_ref[...], g_ref[...])


def _final_norm(h2d, g, *, tm=1024):
    m, d = h2d.shape
    assert m % tm == 0
    return pl.pallas_call(
        _final_norm_kernel,
        grid=(m // tm,),
        in_specs=[pl.BlockSpec((tm, d), lambda i: (i, 0)), pl.BlockSpec((1, d), lambda i: (0, 0))],
        out_specs=pl.BlockSpec((tm, d), lambda i: (i, 0)),
        out_shape=jax.ShapeDtypeStruct((m, d), F32),
        compiler_params=_cparams(1),
        name="final_norm",
    )(h2d, g)


def _block_diag_ones(n):
    i = jnp.arange(n) // HEAD_DIM
    return (i[:, None] == i[None, :]).astype(BF16)


@jax.jit
def _trunk(x, meta, ffn1_norm, ffn1_w13, ffn1_w2, mix_norm, w_in, conv_w, conv_b, conv_ln_g, conv_ln_b, rwkv_mu,
           rwkv_w0, rwkv_wB, rwkv_a0, rwkv_aB, rwkv_gB, rwkv_kk, rwkv_ka, rwkv_rk, rwkv_ln_g, rwkv_ln_b, sb_norm,
           w_out, ffn2_norm, ffn2_w13, ffn2_w2, final_norm):
    bsz, seq, d = x.shape
    depth = w_in.shape[0]
    l_real = N_META + seq
    lp = -(-l_real // SEQ_ALIGN) * SEQ_ALIGN
    meta_b = jnp.broadcast_to(meta.astype(x.dtype)[None], (bsz, N_META, d))
    h = jnp.concatenate([meta_b, x, jnp.zeros((bsz, lp - l_real, d), x.dtype)], axis=1).reshape(bsz * lp, d)

    row3 = lambda p: p.reshape(depth, 1, -1)
    ffn1_w13, ffn1_w2, ffn2_w13, ffn2_w2 = (w.astype(BF16) for w in (ffn1_w13, ffn1_w2, ffn2_w13, ffn2_w2))
    w_in, w_out = w_in.astype(BF16), w_out.astype(BF16)
    ffn1_norm, mix_norm, ffn2_norm, sb_norm = row3(ffn1_norm), row3(mix_norm), row3(ffn2_norm), row3(sb_norm)
    conv_b, conv_ln_g, conv_ln_b = row3(conv_b), row3(conv_ln_g), row3(conv_ln_b)
    rwkv_vecs = [row3(p) for p in (rwkv_mu, rwkv_w0, rwkv_a0, rwkv_kk, rwkv_ka, rwkv_rk, rwkv_ln_g, rwkv_ln_b)]
    mu, w0, a0, k_k, k_a, r_k, ln_g, ln_b = rwkv_vecs
    wb_ext = jnp.pad(rwkv_wB, ((0, 0), (0, AAA_LORA), (0, 0))).astype(BF16)
    ab_ext = jnp.pad(rwkv_aB, ((0, 0), (DECAY_LORA, 0), (0, 0))).astype(BF16)
    gb = rwkv_gB.astype(BF16)
    bd_rwkv = _block_diag_ones(D_RWKV)
    bd_pair = _block_diag_ones(LANES)
    tri = (jnp.arange(RWKV_CHUNK)[:, None] >= jnp.arange(RWKV_CHUNK)[None, :]).astype(BF16)

    for l in range(depth):
        h = _ffn(h, ffn1_norm, ffn1_w13, ffn1_w2, l)
        pc, pr, ps = _inproj(h, mix_norm, w_in, l)
        yc = _conv_group(pc.reshape(bsz, lp, -1), conv_w, conv_b, conv_ln_g, conv_ln_b, l)
        yr = _rwkv_group(pr.reshape(bsz, lp, -1), mu, w0, wb_ext, a0, ab_ext, gb, k_k, k_a, r_k, ln_g, ln_b,
                         bd_rwkv, tri, l)
        ys = _sb_group(ps.reshape(bsz, lp, -1), sb_norm, bd_pair, l)
        h = _outproj(h, yc.reshape(bsz * lp, -1), yr.reshape(bsz * lp, -1), ys.reshape(bsz * lp, -1), w_out, l)
        h = _ffn(h, ffn2_norm, ffn2_w13, ffn2_w2, l)
    out = _final_norm(h, final_norm.reshape(1, d))
    return out.reshape(bsz, lp, d)[:, N_META:l_real]


def kernel(x, meta, ffn1_norm, ffn1_w13, ffn1_w2, mix_norm, w_in, conv_w, conv_b, conv_ln_g, conv_ln_b, rwkv_mu,
           rwkv_w0, rwkv_wB, rwkv_a0, rwkv_aB, rwkv_gB, rwkv_kk, rwkv_ka, rwkv_rk, rwkv_ln_g, rwkv_ln_b, sb_norm,
           w_out, ffn2_norm, ffn2_w13, ffn2_w2, final_norm):
    return _trunk(x, meta, ffn1_norm, ffn1_w13, ffn1_w2, mix_norm, w_in, conv_w, conv_b, conv_ln_g, conv_ln_b,
                  rwkv_mu, rwkv_w0, rwkv_wB, rwkv_a0, rwkv_aB, rwkv_gB, rwkv_kk, rwkv_ka, rwkv_rk, rwkv_ln_g,
                  rwkv_ln_b, sb_norm, w_out, ffn2_norm, ffn2_w13, ffn2_w2, final_norm)
```

```python
import functools

import jax
import jax.numpy as jnp
from jax import lax
from jax.experimental import pallas as pl
from jax.experimental.pallas import tpu as pltpu

F32 = jnp.float32
BF16 = jnp.bfloat16

N_META = 16
HEAD_DIM = 64
D_CONV = 256
CONV_WIDTH = 31
D_RWKV = 384
D_SB = 384
DECAY_LORA = 64
AAA_LORA = 64
GATE_LORA = 128
D_RWKV_IN = 3 * D_RWKV + DECAY_LORA + AAA_LORA + GATE_LORA
RMS_EPS = 1e-6
LN_EPS = 1e-5
GN_EPS = 64e-5

LANES = 128
SEQ_ALIGN = 128
RWKV_CHUNK = 64
VMEM_LIMIT = 56 * 1024 * 1024


def _cparams(n_axes):
    return pltpu.CompilerParams(dimension_semantics=("arbitrary",) * n_axes, vmem_limit_bytes=VMEM_LIMIT)


def _dot(a, b):
    return jnp.dot(a.astype(BF16), b.astype(BF16), preferred_element_type=F32)


def _dot_nt(a, b):
    return lax.dot_general(a.astype(BF16), b.astype(BF16), (((1,), (1,)), ((), ())), preferred_element_type=F32)


def _split3(x):
    h1 = x.astype(BF16)
    r1 = x - h1.astype(F32)
    h2 = r1.astype(BF16)
    return h1, h2, (r1 - h2.astype(F32)).astype(BF16)


def _split2(x):
    hi = x.astype(BF16)
    return hi, (x - hi.astype(F32)).astype(BF16)


def _sigmoid(x):
    return 1.0 / (1.0 + jnp.exp(-x))


def _softplus(x):
    return jnp.maximum(x, 0.0) + jnp.log(1.0 + jnp.exp(-jnp.abs(x)))


def _rms_norm_rows(x, g):
    ms = jnp.mean(x * x, axis=-1, keepdims=True)
    return x * lax.rsqrt(ms + RMS_EPS) * g


def _ffn_kernel(h_ref, g_ref, wg_ref, wu_ref, w2_ref, o_ref, xn_ref, acc_ref, *, n_f):
    f = pl.program_id(1)

    @pl.when(f == 0)
    def _():
        xn_ref[...] = _rms_norm_rows(h_ref[...], g_ref[...]).astype(BF16)
        acc_ref[...] = jnp.zeros_like(acc_ref)

    xn = xn_ref[...]
    gate = jnp.dot(xn, wg_ref[...], preferred_element_type=F32)
    up = jnp.dot(xn, wu_ref[...], preferred_element_type=F32)
    act = (gate * _sigmoid(gate) * up).astype(BF16)
    acc_ref[...] += jnp.dot(act, w2_ref[...], preferred_element_type=F32)

    @pl.when(f == n_f - 1)
    def _():
        o_ref[...] = h_ref[...] + 0.5 * acc_ref[...]


def _ffn(h2d, norm_g, w13, w2, layer, *, tm=1024, tf=256):
    m, d = h2d.shape
    d_ff = w2.shape[1]
    assert m % tm == 0 and d_ff % tf == 0
    n_f = d_ff // tf
    return pl.pallas_call(
        functools.partial(_ffn_kernel, n_f=n_f),
        grid=(m // tm, n_f),
        in_specs=[
            pl.BlockSpec((tm, d), lambda i, f: (i, 0)),
            pl.BlockSpec((None, 1, d), lambda i, f: (layer, 0, 0)),
            pl.BlockSpec((None, d, tf), lambda i, f: (layer, 0, f)),
            pl.BlockSpec((None, d, tf), lambda i, f: (layer, 0, f + n_f)),
            pl.BlockSpec((None, tf, d), lambda i, f: (layer, f, 0)),
        ],
        out_specs=pl.BlockSpec((tm, d), lambda i, f: (i, 0)),
        out_shape=jax.ShapeDtypeStruct((m, d), F32),
        scratch_shapes=[pltpu.VMEM((tm, d), BF16), pltpu.VMEM((tm, d), F32)],
        compiler_params=_cparams(2),
        name="ffn",
    )(h2d, norm_g, w13, w13, w2)


def _inproj_kernel(h_ref, g_ref, w_ref, pc_ref, pr_ref, ps_ref):
    xn = _rms_norm_rows(h_ref[...], g_ref[...]).astype(BF16)
    c0 = 2 * D_CONV
    c1 = c0 + D_RWKV_IN
    pc_ref[...] = jnp.dot(xn, w_ref[:, 0:c0], preferred_element_type=F32)
    pr_ref[...] = jnp.dot(xn, w_ref[:, c0:c1], preferred_element_type=F32)
    ps_ref[...] = jnp.dot(xn, w_ref[:, c1:], preferred_element_type=F32)


def _inproj(h2d, norm_g, w_in, layer, *, tm=512):
    m, d = h2d.shape
    assert m % tm == 0
    d_in = w_in.shape[2]
    widths = (2 * D_CONV, D_RWKV_IN, 3 * D_SB)
    return pl.pallas_call(
        _inproj_kernel,
        grid=(m // tm,),
        in_specs=[
            pl.BlockSpec((tm, d), lambda i: (i, 0)),
            pl.BlockSpec((None, 1, d), lambda i: (layer, 0, 0)),
            pl.BlockSpec((None, d, d_in), lambda i: (layer, 0, 0)),
        ],
        out_specs=[pl.BlockSpec((tm, w), lambda i: (i, 0)) for w in widths],
        out_shape=[jax.ShapeDtypeStruct((m, w), F32) for w in widths],
        compiler_params=_cparams(1),
        name="inproj",
    )(h2d, norm_g, w_in)


CONV_PAD = 32
CONV_SUB = 64


def _conv_kernel(p_ref, w_ref, b_ref, lg_ref, lb_ref, o_ref, buf_ref, *, tt):
    t = pl.program_id(1)

    @pl.when(t == 0)
    def _():
        buf_ref[0:CONV_PAD, :] = jnp.zeros((CONV_PAD, D_CONV), F32)

    val = p_ref[0, :, 0:D_CONV]
    gate = p_ref[0, :, D_CONV:2 * D_CONV]
    buf_ref[CONV_PAD:CONV_PAD + tt, :] = val * _sigmoid(gate)
    first = CONV_PAD - (CONV_WIDTH - 1)
    for s in range(tt // CONV_SUB):
        acc = jnp.zeros((CONV_SUB, D_CONV), F32)
        for j in range(CONV_WIDTH):
            r0 = s * CONV_SUB + first + j
            acc = acc + w_ref[j:j + 1, :] * buf_ref[r0:r0 + CONV_SUB, :]
        y = acc + b_ref[...]
        mu = jnp.mean(y, axis=-1, keepdims=True)
        dlt = y - mu
        var = jnp.mean(dlt * dlt, axis=-1, keepdims=True)
        yn = dlt * lax.rsqrt(var + LN_EPS) * lg_ref[...] + lb_ref[...]
        o_ref[0, s * CONV_SUB:(s + 1) * CONV_SUB, :] = yn * _sigmoid(yn)
    buf_ref[0:CONV_PAD, :] = buf_ref[tt:tt + CONV_PAD, :]


def _conv_group(pc, conv_w, conv_b, ln_g, ln_b, layer, *, tt=128):
    b, lp, _ = pc.shape
    assert lp % tt == 0 and tt % CONV_SUB == 0
    vec = lambda: pl.BlockSpec((None, 1, D_CONV), lambda i, t: (layer, 0, 0))
    return pl.pallas_call(
        functools.partial(_conv_kernel, tt=tt),
        grid=(b, lp // tt),
        in_specs=[
            pl.BlockSpec((1, tt, 2 * D_CONV), lambda i, t: (i, t, 0)),
            pl.BlockSpec((None, CONV_WIDTH, D_CONV), lambda i, t: (layer, 0, 0)),
            vec(), vec(), vec(),
        ],
        out_specs=pl.BlockSpec((1, tt, D_CONV), lambda i, t: (i, t, 0)),
        out_shape=jax.ShapeDtypeStruct((b, lp, D_CONV), F32),
        scratch_shapes=[pltpu.VMEM((tt + CONV_PAD, D_CONV), F32)],
        compiler_params=_cparams(2),
        name="conv_group",
    )(pc, conv_w, conv_b, ln_g, ln_b)


def _rwkv_kernel(p_ref, mu_ref, w0_ref, wb_ref, a0_ref, ab_ref, gb_ref, kk_ref, ka_ref, rk_ref, lg_ref, lb_ref,
                 bd_ref, tri_ref, o_ref, st_ref, carry_ref, *, nb, tt):
    c = RWKV_CHUNK
    c2 = 2 * c
    n_pairs = D_RWKV // LANES
    chunks_per_seq = tt // c
    n_chunks = nb * chunks_per_seq
    t = pl.program_id(1)

    @pl.when(t == 0)
    def _():
        st_ref[...] = jnp.zeros_like(st_ref)
        carry_ref[...] = jnp.zeros_like(carry_ref)

    x = p_ref[...].reshape(nb * tt, D_RWKV_IN)
    row = lax.broadcasted_iota(jnp.int32, x.shape, 0)
    prev = pltpu.roll(x, 1, 0)
    for e in range(nb):
        prev = jnp.where(row == e * tt, carry_ref[e], prev)
        carry_ref[e] = x[(e + 1) * tt - 1:(e + 1) * tt, :]
    xs = x + mu_ref[...] * (prev - x)

    r = xs[:, 0:D_RWKV]
    k = xs[:, D_RWKV:2 * D_RWKV]
    v = xs[:, 2 * D_RWKV:3 * D_RWKV]
    lora_in = xs[:, 3 * D_RWKV:3 * D_RWKV + DECAY_LORA + AAA_LORA]
    gd = xs[:, 3 * D_RWKV + DECAY_LORA + AAA_LORA:]

    bd = bd_ref[...]

    def head_sum(z):
        hi, lo = _split2(z)
        return jnp.dot(hi, bd, preferred_element_type=F32) + jnp.dot(lo, bd, preferred_element_type=F32)

    w_log = -_softplus(-(w0_ref[...] + _dot(jnp.tanh(lora_in), wb_ref[...]))) - 0.5
    logw = -jnp.exp(w_log)
    a = _sigmoid(a0_ref[...] + _dot(lora_in, ab_ref[...]))
    g = _dot(_sigmoid(gd), gb_ref[...])
    kk = k * kk_ref[...]
    kk = kk / jnp.maximum(jnp.sqrt(head_sum(kk * kk)), 1e-12)
    kq = k * (1.0 + (a - 1.0) * ka_ref[...])
    na = -kk
    b = kk * a
    bonus = head_sum(r * kq * rk_ref[...]) * v

    tri = tri_ref[...]
    l1, l2, l3 = _split3(logw)
    cl = (jnp.dot(tri, l1, preferred_element_type=F32) + jnp.dot(tri, l2, preferred_element_type=F32)
          + jnp.dot(tri, l3, preferred_element_type=F32))
    cl_ends = [cl[(s + 1) * c - 1:(s + 1) * c, :] for s in range(n_chunks)]
    cl_end = jnp.concatenate([jnp.broadcast_to(e, (c, D_RWKV)) for e in cl_ends], axis=0)
    e_pos = jnp.exp(cl)
    e_neg = jnp.exp(-cl)
    e_end = jnp.exp(cl_end - cl)
    at_all = jnp.exp(cl - logw) * na
    bt_all = b * e_neg
    kt_all = kq * e_neg
    rt_all = r * e_pos
    bp_all = b * e_end
    kp_all = kq * e_end

    lane = lax.broadcasted_iota(jnp.int32, (c, LANES), 1)
    lane2 = lax.broadcasted_iota(jnp.int32, (c2, LANES), 1)
    m0 = lane < HEAD_DIM
    m0_2 = lane2 < HEAD_DIM
    ri = lax.broadcasted_iota(jnp.int32, (c2, c2), 0)
    cj = lax.broadcasted_iota(jnp.int32, (c2, c2), 1)
    ti = jnp.bitwise_and(ri, c - 1)
    sj = jnp.bitwise_and(cj, c - 1)
    mask_z = (ti > sj) | ((ri >= c) & (ti == sj))
    left_top = lax.broadcasted_iota(jnp.int32, (c, c2), 1) < c
    blockdiag = jnp.bitwise_and(ri, c) == jnp.bitwise_and(cj, c)
    eye = (ri == cj)
    zero = jnp.zeros((c, LANES), F32)
    rows = lambda parts: jnp.concatenate(parts, axis=0)
    cols = lambda parts: jnp.concatenate(parts, axis=1)

    units = [(s, j) for s in range(n_chunks) for j in range(n_pairs)]

    def tile(arr, u):
        s, j = u
        return arr[s * c:(s + 1) * c, j * LANES:(j + 1) * LANES]

    at = [tile(at_all, u) for u in units]
    rt = [tile(rt_all, u) for u in units]
    vv = [tile(v, u) for u in units]

    z0, z1 = [], []
    for i, u in enumerate(units):
        xx = rows([at[i], rt[i]])
        bt, kt = tile(bt_all, u), tile(kt_all, u)
        z0.append(jnp.where(mask_z, _dot_nt(jnp.where(m0_2, xx, 0.0), rows([bt, kt])), 0.0))
        z1.append(jnp.where(mask_z, _dot_nt(jnp.where(m0_2, 0.0, xx), rows([kt, bt])), 0.0))

    pw = [rows([jnp.where(left_top, z0[i][0:c], 0.0), jnp.where(left_top, 0.0, z1[i][0:c])]) for i in range(len(units))]
    tinv = [jnp.where(eye, 1.0, 0.0) + p for p in pw]
    for _ in range(5):
        pw = [_dot(p, p) for p in pw]
        tinv = [ti_ + _dot(ti_, p) for ti_, p in zip(tinv, pw)]

    akv = []
    for i in range(len(units)):
        v2 = rows([vv[i], vv[i]])
        akv.append(jnp.where(m0, _dot(jnp.where(left_top, 0.0, z0[i][0:c]), v2),
                             _dot(jnp.where(left_top, z1[i][0:c], 0.0), v2)))

    ah, uv = [], []
    for i in range(len(units)):
        wa = rows([jnp.where(m0, at[i], 0.0), jnp.where(m0, 0.0, at[i])])
        wv = rows([jnp.where(m0, akv[i], 0.0), jnp.where(m0, 0.0, akv[i])])
        tw = _dot(tinv[i], cols([wa, wv]))
        ah.append(tw[0:c, 0:LANES] + tw[c:c2, 0:LANES])
        uv.append(tw[0:c, LANES:] + tw[c:c2, LANES:])

    rh, ov, gmat, hmat = [], [], [], []
    for i, u in enumerate(units):
        rhs0 = cols([rows([ah[i], zero]), rows([uv[i], vv[i]])])
        rhs1 = cols([rows([zero, ah[i]]), rows([vv[i], uv[i]])])
        res0 = _dot(z0[i][c:c2], rhs0)
        res1 = _dot(z1[i][c:c2], rhs1)
        rh.append(rt[i] + jnp.where(m0, res0[:, 0:LANES], res1[:, 0:LANES]))
        ov.append(jnp.where(m0, res0[:, LANES:], res1[:, LANES:]))
        ypt = rows([tile(bp_all, u), tile(kp_all, u)]).T
        gh = _dot(ypt, rhs0)
        s, j = u
        p_end = jnp.exp(cl_ends[s][:, j * LANES:(j + 1) * LANES])
        gmat.append(jnp.where(blockdiag, gh[:, 0:LANES], 0.0)
                    + jnp.where(eye, jnp.broadcast_to(p_end, (c2, LANES)), 0.0))
        hmat.append(jnp.where(blockdiag, gh[:, LANES:], 0.0))

    states = [st_ref[n] for n in range(nb * n_pairs)]
    out_rows = []
    for s in range(n_chunks):
        out_pairs = []
        for j in range(n_pairs):
            i = s * n_pairs + j
            n = (s // chunks_per_seq) * n_pairs + j
            out_pairs.append(_dot(rh[i], states[n]) + ov[i])
            states[n] = _dot(gmat[i], states[n]) + hmat[i]
        out_rows.append(cols(out_pairs))
    for n in range(nb * n_pairs):
        st_ref[n] = states[n]

    o = rows(out_rows)
    mean = head_sum(o) * (1.0 / HEAD_DIM)
    dlt = o - mean
    var = head_sum(dlt * dlt) * (1.0 / HEAD_DIM)
    on = dlt * lax.rsqrt(var + GN_EPS) * lg_ref[...] + lb_ref[...]
    o_ref[...] = ((on + bonus) * g).reshape(nb, tt, D_RWKV)


def _rwkv_group(pr, mu, w0, wb_ext, a0, ab_ext, gb, k_k, k_a, r_k, ln_g, ln_b, bd, layer, *, nb=2, tt=128):
    b, lp, _ = pr.shape
    assert b % nb == 0 and lp % tt == 0 and tt % RWKV_CHUNK == 0
    n_pairs = D_RWKV // LANES
    idx = jnp.arange(nb * tt)
    tri = ((idx[:, None] >= idx[None, :]) & (idx[:, None] // RWKV_CHUNK == idx[None, :] // RWKV_CHUNK)).astype(BF16)

    def par(arr):
        return pl.BlockSpec((None,) + arr.shape[1:], lambda i, t: (layer,) + (0,) * (arr.ndim - 1))

    def const(arr):
        return pl.BlockSpec(arr.shape, lambda i, t: (0,) * arr.ndim)

    params = (mu, w0, wb_ext, a0, ab_ext, gb, k_k, k_a, r_k, ln_g, ln_b)
    return pl.pallas_call(
        functools.partial(_rwkv_kernel, nb=nb, tt=tt),
        grid=(b // nb, lp // tt),
        in_specs=[pl.BlockSpec((nb, tt, D_RWKV_IN), lambda i, t: (i, t, 0))] + [par(p) for p in params]
        + [const(bd), const(tri)],
        out_specs=pl.BlockSpec((nb, tt, D_RWKV), lambda i, t: (i, t, 0)),
        out_shape=jax.ShapeDtypeStruct((b, lp, D_RWKV), F32),
        scratch_shapes=[pltpu.VMEM((nb * n_pairs, LANES, LANES), F32), pltpu.VMEM((nb, 1, D_RWKV_IN), F32)],
        compiler_params=_cparams(2),
        name="rwkv_group",
    )(pr, *params, bd, tri)


SB_TILE = 128


def _sb_kernel(q_ref, k_ref, v_ref, g_ref, bd_ref, o_ref):
    t = SB_TILE
    n_pairs = D_SB // LANES
    n_heads = 2 * n_pairs
    qi = pl.program_id(1)
    lane = lax.broadcasted_iota(jnp.int32, (t, LANES), 1)
    m0 = lane < HEAD_DIM
    rowi = lax.broadcasted_iota(jnp.int32, (t, t), 0)
    coli = lax.broadcasted_iota(jnp.int32, (t, t), 1)
    later = jnp.where(rowi > coli, 1.0, 0.0).astype(BF16)
    causal = coli < rowi

    qs = []
    for j in range(n_pairs):
        q = q_ref[0, :, j * LANES:(j + 1) * LANES] * (HEAD_DIM ** -0.5)
        qs += [jnp.where(m0, q, 0.0).astype(BF16), jnp.where(m0, 0.0, q).astype(BF16)]

    def block(kb, accs, carries, diagonal):
        off = pl.multiple_of(kb * t, t)
        kbs, vms = [], []
        for j in range(n_pairs):
            kbs.append(k_ref[0, pl.ds(off, t), j * LANES:(j + 1) * LANES].astype(BF16))
            vb = v_ref[0, pl.ds(off, t), j * LANES:(j + 1) * LANES]
            vms += [jnp.where(m0, vb, 0.0).astype(BF16), jnp.where(m0, 0.0, vb).astype(BF16)]
        zs = [lax.dot_general(qs[h], kbs[h // 2], (((1,), (1,)), ((), ())), preferred_element_type=F32)
              for h in range(n_heads)]
        sps = [_softplus(z) for z in zs]
        if diagonal:
            sps = [jnp.where(causal, sp, 0.0) for sp in sps]
        locs = []
        for sp in sps:
            hi, lo = _split2(sp)
            locs.append(jnp.dot(hi, later, preferred_element_type=F32) + jnp.dot(lo, later, preferred_element_type=F32))
        accs = list(accs)
        new_carries = []
        for h in range(n_heads):
            att = jnp.exp(zs[h] - sps[h] - locs[h] - carries[h])
            if diagonal:
                att = jnp.where(causal, att, 0.0)
            accs[h // 2] = accs[h // 2] + jnp.dot(att.astype(BF16), vms[h], preferred_element_type=F32)
            new_carries.append(carries[h] + jnp.sum(sps[h], axis=1, keepdims=True))
        return tuple(accs), tuple(new_carries)

    accs = tuple(jnp.zeros((t, LANES), F32) for _ in range(n_pairs))
    carries = tuple(jnp.zeros((t, 1), F32) for _ in range(n_heads))
    accs, carries = block(qi, accs, carries, True)
    accs, _ = lax.fori_loop(0, qi, lambda it, c: block(qi - 1 - it, c[0], c[1], False), (accs, carries))

    for j in range(n_pairs):
        o = accs[j]
        hi, lo = _split2(o * o)
        ss = jnp.dot(hi, bd_ref[...], preferred_element_type=F32) + jnp.dot(lo, bd_ref[...], preferred_element_type=F32)
        o_ref[0, :, j * LANES:(j + 1) * LANES] = (o * lax.rsqrt(ss * (1.0 / HEAD_DIM) + RMS_EPS)
                                                  * g_ref[:, j * LANES:(j + 1) * LANES])


def _sb_group(ps, norm_g, bd128, layer):
    b, lp, _ = ps.shape
    assert lp % SB_TILE == 0
    return pl.pallas_call(
        _sb_kernel,
        grid=(b, lp // SB_TILE),
        in_specs=[
            pl.BlockSpec((1, SB_TILE, D_SB), lambda i, q: (i, q, 0)),
            pl.BlockSpec((1, lp, D_SB), lambda i, q: (i, 0, 1)),
            pl.BlockSpec((1, lp, D_SB), lambda i, q: (i, 0, 2)),
            pl.BlockSpec((None, 1, D_SB), lambda i, q: (layer, 0, 0)),
            pl.BlockSpec((LANES, LANES), lambda i, q: (0, 0)),
        ],
        out_specs=pl.BlockSpec((1, SB_TILE, D_SB), lambda i, q: (i, q, 0)),
        out_shape=jax.ShapeDtypeStruct((b, lp, D_SB), F32),
        compiler_params=_cparams(2),
        name="sb_group",
    )(ps, ps, ps, norm_g, bd128)


def _outproj_kernel(h_ref, yc_ref, yr_ref, ys_ref, w_ref, o_ref):
    r0 = D_CONV
    r1 = D_CONV + D_RWKV
    acc = jnp.dot(yc_ref[...].astype(BF16), w_ref[0:r0, :], preferred_element_type=F32)
    acc = acc + jnp.dot(yr_ref[...].astype(BF16), w_ref[r0:r1, :], preferred_element_type=F32)
    acc = acc + jnp.dot(ys_ref[...].astype(BF16), w_ref[r1:, :], preferred_element_type=F32)
    o_ref[...] = h_ref[...] + acc


def _outproj(h2d, yc, yr, ys, w_out, layer, *, tm=1024):
    m, d = h2d.shape
    assert m % tm == 0
    rows = lambda w: pl.BlockSpec((tm, w), lambda i: (i, 0))
    return pl.pallas_call(
        _outproj_kernel,
        grid=(m // tm,),
        in_specs=[rows(d), rows(D_CONV), rows(D_RWKV), rows(D_SB),
                  pl.BlockSpec((None,) + w_out.shape[1:], lambda i: (layer, 0, 0))],
        out_specs=rows(d),
        out_shape=jax.ShapeDtypeStruct((m, d), F32),
        compiler_params=_cparams(1),
        name="outproj",
    )(h2d, yc, yr, ys, w_out)


def _final_norm_kernel(h_ref, g_ref, o_ref):
    o_ref[...] = _rms_norm_rows(h_ref[...], g_ref[...])


def _final_norm(h2d, g, *, tm=1024):
    m, d = h2d.shape
    assert m % tm == 0
    return pl.pallas_call(
        _final_norm_kernel,
        grid=(m // tm,),
        in_specs=[pl.BlockSpec((tm, d), lambda i: (i, 0)), pl.BlockSpec((1, d), lambda i: (0, 0))],
        out_specs=pl.BlockSpec((tm, d), lambda i: (i, 0)),
        out_shape=jax.ShapeDtypeStruct((m, d), F32),
        compiler_params=_cparams(1),
        name="final_norm",
    )(h2d, g)


def _block_diag_ones(n):
    i = jnp.arange(n) // HEAD_DIM
    return (i[:, None] == i[None, :]).astype(BF16)


@jax.jit
def _trunk(x, meta, ffn1_norm, ffn1_w13, ffn1_w2, mix_norm, w_in, conv_w, conv_b, conv_ln_g, conv_ln_b, rwkv_mu,
           rwkv_w0, rwkv_wB, rwkv_a0, rwkv_aB, rwkv_gB, rwkv_kk, rwkv_ka, rwkv_rk, rwkv_ln_g, rwkv_ln_b, sb_norm,
           w_out, ffn2_norm, ffn2_w13, ffn2_w2, final_norm):
    bsz, seq, d = x.shape
    depth = w_in.shape[0]
    l_real = N_META + seq
    lp = -(-l_real // SEQ_ALIGN) * SEQ_ALIGN
    meta_b = jnp.broadcast_to(meta.astype(x.dtype)[None], (bsz, N_META, d))
    h = jnp.concatenate([meta_b, x, jnp.zeros((bsz, lp - l_real, d), x.dtype)], axis=1).reshape(bsz * lp, d)

    row3 = lambda p: p.reshape(depth, 1, -1)
    ffn1_w13, ffn1_w2, ffn2_w13, ffn2_w2 = (w.astype(BF16) for w in (ffn1_w13, ffn1_w2, ffn2_w13, ffn2_w2))
    w_in, w_out = w_in.astype(BF16), w_out.astype(BF16)
    ffn1_norm, mix_norm, ffn2_norm, sb_norm = row3(ffn1_norm), row3(mix_norm), row3(ffn2_norm), row3(sb_norm)
    conv_b, conv_ln_g, conv_ln_b = row3(conv_b), row3(conv_ln_g), row3(conv_ln_b)
    rwkv_vecs = [row3(p) for p in (rwkv_mu, rwkv_w0, rwkv_a0, rwkv_kk, rwkv_ka, rwkv_rk, rwkv_ln_g, rwkv_ln_b)]
    mu, w0, a0, k_k, k_a, r_k, ln_g, ln_b = rwkv_vecs
    wb_ext = jnp.pad(rwkv_wB, ((0, 0), (0, AAA_LORA), (0, 0))).astype(BF16)
    ab_ext = jnp.pad(rwkv_aB, ((0, 0), (DECAY_LORA, 0), (0, 0))).astype(BF16)
    gb = rwkv_gB.astype(BF16)
    bd_rwkv = _block_diag_ones(D_RWKV)
    bd_pair = _block_diag_ones(LANES)

    for l in range(depth):
        h = _ffn(h, ffn1_norm, ffn1_w13, ffn1_w2, l)
        pc, pr, ps = _inproj(h, mix_norm, w_in, l)
        yc = _conv_group(pc.reshape(bsz, lp, -1), conv_w, conv_b, conv_ln_g, conv_ln_b, l)
        yr = _rwkv_group(pr.reshape(bsz, lp, -1), mu, w0, wb_ext, a0, ab_ext, gb, k_k, k_a, r_k, ln_g, ln_b,
                         bd_rwkv, l)
        ys = _sb_group(ps.reshape(bsz, lp, -1), sb_norm, bd_pair, l)
        h = _outproj(h, yc.reshape(bsz * lp, -1), yr.reshape(bsz * lp, -1), ys.reshape(bsz * lp, -1), w_out, l)
        h = _ffn(h, ffn2_norm, ffn2_w13, ffn2_w2, l)
    out = _final_norm(h, final_norm.reshape(1, d))
    return out.reshape(bsz, lp, d)[:, N_META:l_real]


def kernel(x, meta, ffn1_norm, ffn1_w13, ffn1_w2, mix_norm, w_in, conv_w, conv_b, conv_ln_g, conv_ln_b, rwkv_mu,
           rwkv_w0, rwkv_wB, rwkv_a0, rwkv_aB, rwkv_gB, rwkv_kk, rwkv_ka, rwkv_rk, rwkv_ln_g, rwkv_ln_b, sb_norm,
           w_out, ffn2_norm, ffn2_w13, ffn2_w2, final_norm):
    return _trunk(x, meta, ffn1_norm, ffn1_w13, ffn1_w2, mix_norm, w_in, conv_w, conv_b, conv_ln_g, conv_ln_b,
                  rwkv_mu, rwkv_w0, rwkv_wB, rwkv_a0, rwkv_aB, rwkv_gB, rwkv_kk, rwkv_ka, rwkv_rk, rwkv_ln_g,
                  rwkv_ln_b, sb_norm, w_out, ffn2_norm, ffn2_w13, ffn2_w2, final_norm)
```

```python
import functools

import jax
import jax.numpy as jnp
from jax import lax
from jax.experimental import pallas as pl
from jax.experimental.pallas import tpu as pltpu

F32 = jnp.float32
BF16 = jnp.bfloat16

N_META = 16
HEAD_DIM = 64
D_CONV = 256
CONV_WIDTH = 31
D_RWKV = 384
D_SB = 384
DECAY_LORA = 64
AAA_LORA = 64
GATE_LORA = 128
D_RWKV_IN = 3 * D_RWKV + DECAY_LORA + AAA_LORA + GATE_LORA
RMS_EPS = 1e-6
LN_EPS = 1e-5
GN_EPS = 64e-5

LANES = 128
SEQ_ALIGN = 128
RWKV_CHUNK = 64
VMEM_LIMIT = 56 * 1024 * 1024


def _cparams(n_axes):
    return pltpu.CompilerParams(dimension_semantics=("arbitrary",) * n_axes, vmem_limit_bytes=VMEM_LIMIT)


def _dot(a, b):
    return jnp.dot(a.astype(BF16), b.astype(BF16), preferred_element_type=F32)


def _dot_nt(a, b):
    return lax.dot_general(a.astype(BF16), b.astype(BF16), (((1,), (1,)), ((), ())), preferred_element_type=F32)


def _split3(x):
    h1 = x.astype(BF16)
    r1 = x - h1.astype(F32)
    h2 = r1.astype(BF16)
    return h1, h2, (r1 - h2.astype(F32)).astype(BF16)


def _split2(x):
    hi = x.astype(BF16)
    return hi, (x - hi.astype(F32)).astype(BF16)


def _sigmoid(x):
    return 1.0 / (1.0 + jnp.exp(-x))


def _neg_abs(x):
    bits = lax.bitcast_convert_type(x, jnp.uint32) | jnp.uint32(0x80000000)
    return lax.bitcast_convert_type(bits, F32)


def _softplus(x):
    return jnp.maximum(x, 0.0) + jnp.log(1.0 + jnp.exp(_neg_abs(x)))


def _rms_norm_rows(x, g):
    ms = jnp.mean(x * x, axis=-1, keepdims=True)
    return x * lax.rsqrt(ms + RMS_EPS) * g


def _ffn_kernel(h_ref, g_ref, wg_ref, wu_ref, w2_ref, o_ref, xn_ref, acc_ref, *, n_f):
    f = pl.program_id(1)

    @pl.when(f == 0)
    def _():
        xn_ref[...] = _rms_norm_rows(h_ref[...], g_ref[...]).astype(BF16)
        acc_ref[...] = jnp.zeros_like(acc_ref)

    xn = xn_ref[...]
    gate = jnp.dot(xn, wg_ref[...], preferred_element_type=F32)
    up = jnp.dot(xn, wu_ref[...], preferred_element_type=F32)
    act = (gate * _sigmoid(gate) * up).astype(BF16)
    acc_ref[...] += jnp.dot(act, w2_ref[...], preferred_element_type=F32)

    @pl.when(f == n_f - 1)
    def _():
        o_ref[...] = h_ref[...] + 0.5 * acc_ref[...]


def _ffn(h2d, norm_g, w13, w2, layer, *, tm=1024, tf=256):
    m, d = h2d.shape
    d_ff = w2.shape[1]
    assert m % tm == 0 and d_ff % tf == 0
    n_f = d_ff // tf
    return pl.pallas_call(
        functools.partial(_ffn_kernel, n_f=n_f),
        grid=(m // tm, n_f),
        in_specs=[
            pl.BlockSpec((tm, d), lambda i, f: (i, 0)),
            pl.BlockSpec((None, 1, d), lambda i, f: (layer, 0, 0)),
            pl.BlockSpec((None, d, tf), lambda i, f: (layer, 0, f)),
            pl.BlockSpec((None, d, tf), lambda i, f: (layer, 0, f + n_f)),
            pl.BlockSpec((None, tf, d), lambda i, f: (layer, f, 0)),
        ],
        out_specs=pl.BlockSpec((tm, d), lambda i, f: (i, 0)),
        out_shape=jax.ShapeDtypeStruct((m, d), F32),
        scratch_shapes=[pltpu.VMEM((tm, d), BF16), pltpu.VMEM((tm, d), F32)],
        compiler_params=_cparams(2),
        name="ffn",
    )(h2d, norm_g, w13, w13, w2)


def _inproj_kernel(h_ref, g_ref, w_ref, pc_ref, pr_ref, ps_ref):
    xn = _rms_norm_rows(h_ref[...], g_ref[...]).astype(BF16)
    c0 = 2 * D_CONV
    c1 = c0 + D_RWKV_IN
    pc_ref[...] = jnp.dot(xn, w_ref[:, 0:c0], preferred_element_type=F32)
    pr_ref[...] = jnp.dot(xn, w_ref[:, c0:c1], preferred_element_type=F32)
    ps_ref[...] = jnp.dot(xn, w_ref[:, c1:], preferred_element_type=F32)


def _inproj(h2d, norm_g, w_in, layer, *, tm=512):
    m, d = h2d.shape
    assert m % tm == 0
    d_in = w_in.shape[2]
    widths = (2 * D_CONV, D_RWKV_IN, 3 * D_SB)
    return pl.pallas_call(
        _inproj_kernel,
        grid=(m // tm,),
        in_specs=[
            pl.BlockSpec((tm, d), lambda i: (i, 0)),
            pl.BlockSpec((None, 1, d), lambda i: (layer, 0, 0)),
            pl.BlockSpec((None, d, d_in), lambda i: (layer, 0, 0)),
        ],
        out_specs=[pl.BlockSpec((tm, w), lambda i: (i, 0)) for w in widths],
        out_shape=[jax.ShapeDtypeStruct((m, w), F32) for w in widths],
        compiler_params=_cparams(1),
        name="inproj",
    )(h2d, norm_g, w_in)


CONV_PAD = 32
CONV_SUB = 64


def _conv_kernel(p_ref, w_ref, b_ref, lg_ref, lb_ref, o_ref, buf_ref, *, tt):
    t = pl.program_id(1)

    @pl.when(t == 0)
    def _():
        buf_ref[0:CONV_PAD, :] = jnp.zeros((CONV_PAD, D_CONV), F32)

    val = p_ref[0, :, 0:D_CONV]
    gate = p_ref[0, :, D_CONV:2 * D_CONV]
    buf_ref[CONV_PAD:CONV_PAD + tt, :] = val * _sigmoid(gate)
    first = CONV_PAD - (CONV_WIDTH - 1)
    for s in range(tt // CONV_SUB):
        acc = jnp.zeros((CONV_SUB, D_CONV), F32)
        for j in range(CONV_WIDTH):
            r0 = s * CONV_SUB + first + j
            acc = acc + w_ref[j:j + 1, :] * buf_ref[r0:r0 + CONV_SUB, :]
        y = acc + b_ref[...]
        mu = jnp.mean(y, axis=-1, keepdims=True)
        dlt = y - mu
        var = jnp.mean(dlt * dlt, axis=-1, keepdims=True)
        yn = dlt * lax.rsqrt(var + LN_EPS) * lg_ref[...] + lb_ref[...]
        o_ref[0, s * CONV_SUB:(s + 1) * CONV_SUB, :] = yn * _sigmoid(yn)
    buf_ref[0:CONV_PAD, :] = buf_ref[tt:tt + CONV_PAD, :]


def _conv_group(pc, conv_w, conv_b, ln_g, ln_b, layer, *, tt=128):
    b, lp, _ = pc.shape
    assert lp % tt == 0 and tt % CONV_SUB == 0
    vec = lambda: pl.BlockSpec((None, 1, D_CONV), lambda i, t: (layer, 0, 0))
    return pl.pallas_call(
        functools.partial(_conv_kernel, tt=tt),
        grid=(b, lp // tt),
        in_specs=[
            pl.BlockSpec((1, tt, 2 * D_CONV), lambda i, t: (i, t, 0)),
            pl.BlockSpec((None, CONV_WIDTH, D_CONV), lambda i, t: (layer, 0, 0)),
            vec(), vec(), vec(),
        ],
        out_specs=pl.BlockSpec((1, tt, D_CONV), lambda i, t: (i, t, 0)),
        out_shape=jax.ShapeDtypeStruct((b, lp, D_CONV), F32),
        scratch_shapes=[pltpu.VMEM((tt + CONV_PAD, D_CONV), F32)],
        compiler_params=_cparams(2),
        name="conv_group",
    )(pc, conv_w, conv_b, ln_g, ln_b)


def _rwkv_kernel(p_ref, mu_ref, w0_ref, wb_ref, a0_ref, ab_ref, gb_ref, kk_ref, ka_ref, rk_ref, lg_ref, lb_ref,
                 bd_ref, tri_ref, o_ref, st_ref, carry_ref, *, nb, tt):
    c = RWKV_CHUNK
    c2 = 2 * c
    n_pairs = D_RWKV // LANES
    chunks_per_seq = tt // c
    n_chunks = nb * chunks_per_seq
    t = pl.program_id(1)

    @pl.when(t == 0)
    def _():
        st_ref[...] = jnp.zeros_like(st_ref)
        carry_ref[...] = jnp.zeros_like(carry_ref)

    x = p_ref[...].reshape(nb * tt, D_RWKV_IN)
    row = lax.broadcasted_iota(jnp.int32, x.shape, 0)
    prev = pltpu.roll(x, 1, 0)
    for e in range(nb):
        prev = jnp.where(row == e * tt, carry_ref[e], prev)
        carry_ref[e] = x[(e + 1) * tt - 1:(e + 1) * tt, :]
    xs = x + mu_ref[...] * (prev - x)

    r = xs[:, 0:D_RWKV]
    k = xs[:, D_RWKV:2 * D_RWKV]
    v = xs[:, 2 * D_RWKV:3 * D_RWKV]
    lora_in = xs[:, 3 * D_RWKV:3 * D_RWKV + DECAY_LORA + AAA_LORA]
    gd = xs[:, 3 * D_RWKV + DECAY_LORA + AAA_LORA:]

    bd = bd_ref[...]

    def head_sum(z):
        hi, lo = _split2(z)
        return jnp.dot(hi, bd, preferred_element_type=F32) + jnp.dot(lo, bd, preferred_element_type=F32)

    w_log = -_softplus(-(w0_ref[...] + _dot(jnp.tanh(lora_in), wb_ref[...]))) - 0.5
    logw = -jnp.exp(w_log)
    a = _sigmoid(a0_ref[...] + _dot(lora_in, ab_ref[...]))
    g = _dot(_sigmoid(gd), gb_ref[...])
    kk = k * kk_ref[...]
    kk = kk / jnp.maximum(jnp.sqrt(head_sum(kk * kk)), 1e-12)
    kq = k * (1.0 + (a - 1.0) * ka_ref[...])
    na = -kk
    b = kk * a
    bonus = head_sum(r * kq * rk_ref[...]) * v

    tri = tri_ref[...]
    l1, l2, l3 = _split3(logw)
    cl = (jnp.dot(tri, l1, preferred_element_type=F32) + jnp.dot(tri, l2, preferred_element_type=F32)
          + jnp.dot(tri, l3, preferred_element_type=F32))
    cl_ends = [cl[(s + 1) * c - 1:(s + 1) * c, :] for s in range(n_chunks)]
    cl_end = jnp.concatenate([jnp.broadcast_to(e, (c, D_RWKV)) for e in cl_ends], axis=0)
    e_pos = jnp.exp(cl)
    e_neg = jnp.exp(-cl)
    e_end = jnp.exp(cl_end - cl)
    at_all = jnp.exp(cl - logw) * na
    bt_all = b * e_neg
    kt_all = kq * e_neg
    rt_all = r * e_pos
    bp_all = b * e_end
    kp_all = kq * e_end

    lane = lax.broadcasted_iota(jnp.int32, (c, LANES), 1)
    lane2 = lax.broadcasted_iota(jnp.int32, (c2, LANES), 1)
    m0 = lane < HEAD_DIM
    m0_2 = lane2 < HEAD_DIM
    ri = lax.broadcasted_iota(jnp.int32, (c2, c2), 0)
    cj = lax.broadcasted_iota(jnp.int32, (c2, c2), 1)
    ti = jnp.bitwise_and(ri, c - 1)
    sj = jnp.bitwise_and(cj, c - 1)
    mask_z = (ti > sj) | ((ri >= c) & (ti == sj))
    left_top = lax.broadcasted_iota(jnp.int32, (c, c2), 1) < c
    blockdiag = jnp.bitwise_and(ri, c) == jnp.bitwise_and(cj, c)
    eye = (ri == cj)
    zero = jnp.zeros((c, LANES), F32)
    rows = lambda parts: jnp.concatenate(parts, axis=0)
    cols = lambda parts: jnp.concatenate(parts, axis=1)

    units = [(s, j) for s in range(n_chunks) for j in range(n_pairs)]

    def tile(arr, u):
        s, j = u
        return arr[s * c:(s + 1) * c, j * LANES:(j + 1) * LANES]

    at = [tile(at_all, u) for u in units]
    rt = [tile(rt_all, u) for u in units]
    vv = [tile(v, u) for u in units]

    z0, z1 = [], []
    for i, u in enumerate(units):
        xx = rows([at[i], rt[i]])
        bt, kt = tile(bt_all, u), tile(kt_all, u)
        z0.append(jnp.where(mask_z, _dot_nt(jnp.where(m0_2, xx, 0.0), rows([bt, kt])), 0.0))
        z1.append(jnp.where(mask_z, _dot_nt(jnp.where(m0_2, 0.0, xx), rows([kt, bt])), 0.0))

    pw = [rows([jnp.where(left_top, z0[i][0:c], 0.0), jnp.where(left_top, 0.0, z1[i][0:c])]) for i in range(len(units))]
    tinv = [jnp.where(eye, 1.0, 0.0) + p for p in pw]
    for _ in range(5):
        pw = [_dot(p, p) for p in pw]
        tinv = [ti_ + _dot(ti_, p) for ti_, p in zip(tinv, pw)]

    akv = []
    for i in range(len(units)):
        v2 = rows([vv[i], vv[i]])
        akv.append(jnp.where(m0, _dot(jnp.where(left_top, 0.0, z0[i][0:c]), v2),
                             _dot(jnp.where(left_top, z1[i][0:c], 0.0), v2)))

    ah, uv = [], []
    for i in range(len(units)):
        wa = rows([jnp.where(m0, at[i], 0.0), jnp.where(m0, 0.0, at[i])])
        wv = rows([jnp.where(m0, akv[i], 0.0), jnp.where(m0, 0.0, akv[i])])
        tw = _dot(tinv[i], cols([wa, wv]))
        ah.append(tw[0:c, 0:LANES] + tw[c:c2, 0:LANES])
        uv.append(tw[0:c, LANES:] + tw[c:c2, LANES:])

    rh, ov, gmat, hmat = [], [], [], []
    for i, u in enumerate(units):
        rhs0 = cols([rows([ah[i], zero]), rows([uv[i], vv[i]])])
        rhs1 = cols([rows([zero, ah[i]]), rows([vv[i], uv[i]])])
        res0 = _dot(z0[i][c:c2], rhs0)
        res1 = _dot(z1[i][c:c2], rhs1)
        rh.append(rt[i] + jnp.where(m0, res0[:, 0:LANES], res1[:, 0:LANES]))
        ov.append(jnp.where(m0, res0[:, LANES:], res1[:, LANES:]))
        ypt = rows([tile(bp_all, u), tile(kp_all, u)]).T
        gh = _dot(ypt, rhs0)
        s, j = u
        p_end = jnp.exp(cl_ends[s][:, j * LANES:(j + 1) * LANES])
        gmat.append(jnp.where(blockdiag, gh[:, 0:LANES], 0.0)
                    + jnp.where(eye, jnp.broadcast_to(p_end, (c2, LANES)), 0.0))
        hmat.append(jnp.where(blockdiag, gh[:, LANES:], 0.0))

    states = [st_ref[n] for n in range(nb * n_pairs)]
    out_rows = []
    for s in range(n_chunks):
        out_pairs = []
        for j in range(n_pairs):
            i = s * n_pairs + j
            n = (s // chunks_per_seq) * n_pairs + j
            out_pairs.append(_dot(rh[i], states[n]) + ov[i])
            states[n] = _dot(gmat[i], states[n]) + hmat[i]
        out_rows.append(cols(out_pairs))
    for n in range(nb * n_pairs):
        st_ref[n] = states[n]

    o = rows(out_rows)
    mean = head_sum(o) * (1.0 / HEAD_DIM)
    dlt = o - mean
    var = head_sum(dlt * dlt) * (1.0 / HEAD_DIM)
    on = dlt * lax.rsqrt(var + GN_EPS) * lg_ref[...] + lb_ref[...]
    o_ref[...] = ((on + bonus) * g).reshape(nb, tt, D_RWKV)


def _rwkv_group(pr, mu, w0, wb_ext, a0, ab_ext, gb, k_k, k_a, r_k, ln_g, ln_b, bd, layer, *, nb=2, tt=128):
    b, lp, _ = pr.shape
    assert b % nb == 0 and lp % tt == 0 and tt % RWKV_CHUNK == 0
    n_pairs = D_RWKV // LANES
    idx = jnp.arange(nb * tt)
    tri = ((idx[:, None] >= idx[None, :]) & (idx[:, None] // RWKV_CHUNK == idx[None, :] // RWKV_CHUNK)).astype(BF16)

    def par(arr):
        return pl.BlockSpec((None,) + arr.shape[1:], lambda i, t: (layer,) + (0,) * (arr.ndim - 1))

    def const(arr):
        return pl.BlockSpec(arr.shape, lambda i, t: (0,) * arr.ndim)

    params = (mu, w0, wb_ext, a0, ab_ext, gb, k_k, k_a, r_k, ln_g, ln_b)
    return pl.pallas_call(
        functools.partial(_rwkv_kernel, nb=nb, tt=tt),
        grid=(b // nb, lp // tt),
        in_specs=[pl.BlockSpec((nb, tt, D_RWKV_IN), lambda i, t: (i, t, 0))] + [par(p) for p in params]
        + [const(bd), const(tri)],
        out_specs=pl.BlockSpec((nb, tt, D_RWKV), lambda i, t: (i, t, 0)),
        out_shape=jax.ShapeDtypeStruct((b, lp, D_RWKV), F32),
        scratch_shapes=[pltpu.VMEM((nb * n_pairs, LANES, LANES), F32), pltpu.VMEM((nb, 1, D_RWKV_IN), F32)],
        compiler_params=_cparams(2),
        name="rwkv_group",
    )(pr, *params, bd, tri)


SB_TILE = 128


def _sb_kernel(q_ref, k_ref, v_ref, g_ref, bd_ref, later_ref, o_ref, km_ref, vm_ref):
    t = SB_TILE
    n_pairs = D_SB // LANES
    qi = pl.program_id(1)

    @pl.when(qi == 0)
    def _():
        m0 = lax.broadcasted_iota(jnp.int32, (k_ref.shape[1], LANES), 1) < HEAD_DIM
        for j in range(n_pairs):
            kj = k_ref[0, :, j * LANES:(j + 1) * LANES]
            vj = v_ref[0, :, j * LANES:(j + 1) * LANES]
            km_ref[2 * j] = jnp.where(m0, kj, 0.0).astype(BF16)
            km_ref[2 * j + 1] = jnp.where(m0, 0.0, kj).astype(BF16)
            vm_ref[2 * j] = jnp.where(m0, vj, 0.0).astype(BF16)
            vm_ref[2 * j + 1] = jnp.where(m0, 0.0, vj).astype(BF16)

    rowi = lax.broadcasted_iota(jnp.int32, (t, t), 0)
    coli = lax.broadcasted_iota(jnp.int32, (t, t), 1)
    causal = coli < rowi
    causal2 = jnp.concatenate([causal, causal], axis=1)
    later2 = later_ref[...]
    qb = [(q_ref[0, :, j * LANES:(j + 1) * LANES] * (HEAD_DIM ** -0.5)).astype(BF16) for j in range(n_pairs)]

    def both_heads(ref, kb, j):
        off = kb * t if isinstance(kb, int) else pl.multiple_of(kb * t, t)
        return jnp.concatenate([ref[2 * j, pl.ds(off, t), :], ref[2 * j + 1, pl.ds(off, t), :]], axis=0)

    def scores(kbs):
        return tuple(lax.dot_general(qb[j], both_heads(km_ref, kb, j), (((1,), (1,)), ((), ())),
                                     preferred_element_type=F32) for kb in kbs for j in range(n_pairs))

    def group(kbs, zs, accs, carries, diagonal):
        units = [(kb, j) for kb in kbs for j in range(n_pairs)]
        sps = [_softplus(z) for z in zs]
        if diagonal:
            sps = [jnp.where(causal2, sp, 0.0) for sp in sps]
        locs = [jnp.dot(jnp.concatenate(_split2(sp), axis=1), later2, preferred_element_type=F32) for sp in sps]
        accs, carries = list(accs), list(carries)
        for i, (kb, j) in enumerate(units):
            halves = []
            for m in range(2):
                sl = slice(m * t, (m + 1) * t)
                att = jnp.exp(zs[i][:, sl] - locs[i][:, sl] - carries[2 * j + m])
                if diagonal:
                    att = jnp.where(causal, att, 0.0)
                halves.append(att.astype(BF16))
                carries[2 * j + m] = carries[2 * j + m] + jnp.sum(sps[i][:, sl], axis=1, keepdims=True)
            accs[j] = accs[j] + jnp.dot(jnp.concatenate(halves, axis=1), both_heads(vm_ref, kb, j),
                                        preferred_element_type=F32)
        return tuple(accs), tuple(carries)

    state = (tuple(jnp.zeros((t, LANES), F32) for _ in range(n_pairs)),
             tuple(jnp.zeros((t, 1), F32) for _ in range(2 * n_pairs)))
    state = group([qi], scores([qi]), *state, True)

    def run(first, n_iters, width, state):
        def body(it, c):
            kbs = [first - width * it - w for w in range(width)]
            return group(kbs, scores(kbs), *c, False)
        return lax.fori_loop(0, n_iters, body, state)

    n4 = jnp.right_shift(qi, 2)
    n2 = jnp.bitwise_and(jnp.right_shift(qi, 1), 1)
    state = run(qi - 1, n4, 4, state)
    state = run(qi - 1 - 4 * n4, n2, 2, state)
    state = run(0, jnp.bitwise_and(qi, 1), 1, state)
    accs = state[0]

    for j in range(n_pairs):
        o = accs[j]
        hi, lo = _split2(o * o)
        ss = jnp.dot(hi, bd_ref[...], preferred_element_type=F32) + jnp.dot(lo, bd_ref[...], preferred_element_type=F32)
        o_ref[0, :, j * LANES:(j + 1) * LANES] = (o * lax.rsqrt(ss * (1.0 / HEAD_DIM) + RMS_EPS)
                                                  * g_ref[:, j * LANES:(j + 1) * LANES])


def _sb_group(ps, norm_g, bd128, layer):
    b, lp, _ = ps.shape
    assert lp % SB_TILE == 0
    n_pairs = D_SB // LANES
    idx = jnp.arange(2 * SB_TILE)
    later2 = ((idx[:, None] >= idx[None, :]) & (idx[:, None] // SB_TILE == idx[None, :] // SB_TILE)).astype(BF16)
    later2 = jnp.concatenate([later2, later2], axis=0)
    return pl.pallas_call(
        _sb_kernel,
        grid=(b, lp // SB_TILE),
        in_specs=[
            pl.BlockSpec((1, SB_TILE, D_SB), lambda i, q: (i, q, 0)),
            pl.BlockSpec((1, lp, D_SB), lambda i, q: (i, 0, 1)),
            pl.BlockSpec((1, lp, D_SB), lambda i, q: (i, 0, 2)),
            pl.BlockSpec((None, 1, D_SB), lambda i, q: (layer, 0, 0)),
            pl.BlockSpec((LANES, LANES), lambda i, q: (0, 0)),
            pl.BlockSpec(later2.shape, lambda i, q: (0, 0)),
        ],
        out_specs=pl.BlockSpec((1, SB_TILE, D_SB), lambda i, q: (i, q, 0)),
        out_shape=jax.ShapeDtypeStruct((b, lp, D_SB), F32),
        scratch_shapes=[pltpu.VMEM((2 * n_pairs, lp, LANES), BF16), pltpu.VMEM((2 * n_pairs, lp, LANES), BF16)],
        compiler_params=_cparams(2),
        name="sb_group",
    )(ps, ps, ps, norm_g, bd128, later2)


def _outproj_kernel(h_ref, yc_ref, yr_ref, ys_ref, w_ref, o_ref):
    r0 = D_CONV
    r1 = D_CONV + D_RWKV
    acc = jnp.dot(yc_ref[...].astype(BF16), w_ref[0:r0, :], preferred_element_type=F32)
    acc = acc + jnp.dot(yr_ref[...].astype(BF16), w_ref[r0:r1, :], preferred_element_type=F32)
    acc = acc + jnp.dot(ys_ref[...].astype(BF16), w_ref[r1:, :], preferred_element_type=F32)
    o_ref[...] = h_ref[...] + acc


def _outproj(h2d, yc, yr, ys, w_out, layer, *, tm=1024):
    m, d = h2d.shape
    assert m % tm == 0
    rows = lambda w: pl.BlockSpec((tm, w), lambda i: (i, 0))
    return pl.pallas_call(
        _outproj_kernel,
        grid=(m // tm,),
        in_specs=[rows(d), rows(D_CONV), rows(D_RWKV), rows(D_SB),
                  pl.BlockSpec((None,) + w_out.shape[1:], lambda i: (layer, 0, 0))],
        out_specs=rows(d),
        out_shape=jax.ShapeDtypeStruct((m, d), F32),
        compiler_params=_cparams(1),
        name="outproj",
    )(h2d, yc, yr, ys, w_out)


def _final_norm_kernel(h_ref, g_ref, o_ref):
    o_ref[...] = _rms_norm_rows(h_ref[...], g_ref[...])


def _final_norm(h2d, g, *, tm=1024):
    m, d = h2d.shape
    assert m % tm == 0
    return pl.pallas_call(
        _final_norm_kernel,
        grid=(m // tm,),
        in_specs=[pl.BlockSpec((tm, d), lambda i: (i, 0)), pl.BlockSpec((1, d), lambda i: (0, 0))],
        out_specs=pl.BlockSpec((tm, d), lambda i: (i, 0)),
        out_shape=jax.ShapeDtypeStruct((m, d), F32),
        compiler_params=_cparams(1),
        name="final_norm",
    )(h2d, g)


def _block_diag_ones(n):
    i = jnp.arange(n) // HEAD_DIM
    return (i[:, None] == i[None, :]).astype(BF16)


@jax.jit
def _trunk(x, meta, ffn1_norm, ffn1_w13, ffn1_w2, mix_norm, w_in, conv_w, conv_b, conv_ln_g, conv_ln_b, rwkv_mu,
           rwkv_w0, rwkv_wB, rwkv_a0, rwkv_aB, rwkv_gB, rwkv_kk, rwkv_ka, rwkv_rk, rwkv_ln_g, rwkv_ln_b, sb_norm,
           w_out, ffn2_norm, ffn2_w13, ffn2_w2, final_norm):
    bsz, seq, d = x.shape
    depth = w_in.shape[0]
    l_real = N_META + seq
    lp = -(-l_real // SEQ_ALIGN) * SEQ_ALIGN
    meta_b = jnp.broadcast_to(meta.astype(x.dtype)[None], (bsz, N_META, d))
    h = jnp.concatenate([meta_b, x, jnp.zeros((bsz, lp - l_real, d), x.dtype)], axis=1).reshape(bsz * lp, d)

    row3 = lambda p: p.reshape(depth, 1, -1)
    ffn1_w13, ffn1_w2, ffn2_w13, ffn2_w2 = (w.astype(BF16) for w in (ffn1_w13, ffn1_w2, ffn2_w13, ffn2_w2))
    w_in, w_out = w_in.astype(BF16), w_out.astype(BF16)
    ffn1_norm, mix_norm, ffn2_norm, sb_norm = row3(ffn1_norm), row3(mix_norm), row3(ffn2_norm), row3(sb_norm)
    conv_b, conv_ln_g, conv_ln_b = row3(conv_b), row3(conv_ln_g), row3(conv_ln_b)
    rwkv_vecs = [row3(p) for p in (rwkv_mu, rwkv_w0, rwkv_a0, rwkv_kk, rwkv_ka, rwkv_rk, rwkv_ln_g, rwkv_ln_b)]
    mu, w0, a0, k_k, k_a, r_k, ln_g, ln_b = rwkv_vecs
    wb_ext = jnp.pad(rwkv_wB, ((0, 0), (0, AAA_LORA), (0, 0))).astype(BF16)
    ab_ext = jnp.pad(rwkv_aB, ((0, 0), (DECAY_LORA, 0), (0, 0))).astype(BF16)
    gb = rwkv_gB.astype(BF16)
    bd_rwkv = _block_diag_ones(D_RWKV)
    bd_pair = _block_diag_ones(LANES)

    for l in range(depth):
        h = _ffn(h, ffn1_norm, ffn1_w13, ffn1_w2, l)
        pc, pr, ps = _inproj(h, mix_norm, w_in, l)
        yc = _conv_group(pc.reshape(bsz, lp, -1), conv_w, conv_b, conv_ln_g, conv_ln_b, l)
        yr = _rwkv_group(pr.reshape(bsz, lp, -1), mu, w0, wb_ext, a0, ab_ext, gb, k_k, k_a, r_k, ln_g, ln_b,
                         bd_rwkv, l)
        ys = _sb_group(ps.reshape(bsz, lp, -1), sb_norm, bd_pair, l)
        h = _outproj(h, yc.reshape(bsz * lp, -1), yr.reshape(bsz * lp, -1), ys.reshape(bsz * lp, -1), w_out, l)
        h = _ffn(h, ffn2_norm, ffn2_w13, ffn2_w2, l)
    out = _final_norm(h, final_norm.reshape(1, d))
    return out.reshape(bsz, lp, d)[:, N_META:l_real]


def kernel(x, meta, ffn1_norm, ffn1_w13, ffn1_w2, mix_norm, w_in, conv_w, conv_b, conv_ln_g, conv_ln_b, rwkv_mu,
           rwkv_w0, rwkv_wB, rwkv_a0, rwkv_aB, rwkv_gB, rwkv_kk, rwkv_ka, rwkv_rk, rwkv_ln_g, rwkv_ln_b, sb_norm,
           w_out, ffn2_norm, ffn2_w13, ffn2_w2, final_norm):
    return _trunk(x, meta, ffn1_norm, ffn1_w13, ffn1_w2, mix_norm, w_in, conv_w, conv_b, conv_ln_g, conv_ln_b,
                  rwkv_mu, rwkv_w0, rwkv_wB, rwkv_a0, rwkv_aB, rwkv_gB, rwkv_kk, rwkv_ka, rwkv_rk, rwkv_ln_g,
                  rwkv_ln_b, sb_norm, w_out, ffn2_norm, ffn2_w13, ffn2_w2, final_norm)
```

```python
import functools

import jax
import jax.numpy as jnp
from jax import lax
from jax.experimental import pallas as pl
from jax.experimental.pallas import tpu as pltpu

F32 = jnp.float32
BF16 = jnp.bfloat16

N_META = 16
HEAD_DIM = 64
D_CONV = 256
CONV_WIDTH = 31
D_RWKV = 384
D_SB = 384
DECAY_LORA = 64
AAA_LORA = 64
GATE_LORA = 128
D_RWKV_IN = 3 * D_RWKV + DECAY_LORA + AAA_LORA + GATE_LORA
RMS_EPS = 1e-6
LN_EPS = 1e-5
GN_EPS = 64e-5

LANES = 128
SEQ_ALIGN = 128
RWKV_CHUNK = 64
VMEM_LIMIT = 56 * 1024 * 1024


def _cparams(n_axes):
    return pltpu.CompilerParams(dimension_semantics=("arbitrary",) * n_axes, vmem_limit_bytes=VMEM_LIMIT)


def _dot(a, b):
    return jnp.dot(a.astype(BF16), b.astype(BF16), preferred_element_type=F32)


def _dot_nt(a, b):
    return lax.dot_general(a.astype(BF16), b.astype(BF16), (((1,), (1,)), ((), ())), preferred_element_type=F32)


def _split3(x):
    h1 = x.astype(BF16)
    r1 = x - h1.astype(F32)
    h2 = r1.astype(BF16)
    return h1, h2, (r1 - h2.astype(F32)).astype(BF16)


def _split2(x):
    hi = x.astype(BF16)
    return hi, (x - hi.astype(F32)).astype(BF16)


def _sigmoid(x):
    return 1.0 / (1.0 + jnp.exp(-x))


def _softplus(x):
    return jnp.maximum(x, 0.0) + jnp.log(1.0 + jnp.exp(-jnp.abs(x)))


def _rms_norm_rows(x, g):
    ms = jnp.mean(x * x, axis=-1, keepdims=True)
    return x * lax.rsqrt(ms + RMS_EPS) * g


def _ffn_kernel(h_ref, yc_ref, yr_ref, ys_ref, wo_ref, g_ref, w13_ref, w2_ref, fg_ref, o_ref, act_ref, *,
                d_ff, tf, mix, final):
    hin = h_ref[...]
    if mix:
        r0 = D_CONV
        r1 = D_CONV + D_RWKV
        hin = hin + jnp.dot(yc_ref[...], wo_ref[0:r0, :], preferred_element_type=F32)
        hin = hin + jnp.dot(yr_ref[...], wo_ref[r0:r1, :], preferred_element_type=F32)
        hin = hin + jnp.dot(ys_ref[...], wo_ref[r1:, :], preferred_element_type=F32)
    o_ref[...] = hin
    xn = _rms_norm_rows(hin, g_ref[...]).astype(BF16)
    for c0 in range(0, d_ff, tf):
        gate = jnp.dot(xn, w13_ref[:, c0:c0 + tf], preferred_element_type=F32)
        up = jnp.dot(xn, w13_ref[:, d_ff + c0:d_ff + c0 + tf], preferred_element_type=F32)
        act_ref[:, c0:c0 + tf] = (gate * _sigmoid(gate) * up).astype(BF16)
    out = o_ref[...] + 0.5 * jnp.dot(act_ref[...], w2_ref[...], preferred_element_type=F32)
    if final:
        out = _rms_norm_rows(out, fg_ref[...])
    o_ref[...] = out


def _ffn(h2d, norm_g, w13, w2, layer, *, mix=None, final_g=None, tm=512, tf=256):
    m, d = h2d.shape
    d_ff = w2.shape[1]
    assert m % tm == 0 and d_ff % tf == 0
    resident = dict(pipeline_mode=pl.Buffered(1))
    rows = lambda w: pl.BlockSpec((tm, w), lambda i: (i, 0))
    if mix is None:
        mix_args = [jnp.zeros((tm, w), BF16) for w in (D_CONV, D_RWKV, D_SB)] + [jnp.zeros((1, 16, d), BF16)]
        mix_specs = [pl.BlockSpec((tm, w), lambda i: (0, 0)) for w in (D_CONV, D_RWKV, D_SB)]
        wo_layer = 0
    else:
        mix_args = list(mix)
        mix_specs = [rows(D_CONV), rows(D_RWKV), rows(D_SB)]
        wo_layer = layer
    mix_specs.append(pl.BlockSpec((None,) + mix_args[3].shape[1:], lambda i: (wo_layer, 0, 0), **resident))
    fg = jnp.zeros((1, d), F32) if final_g is None else final_g
    return pl.pallas_call(
        functools.partial(_ffn_kernel, d_ff=d_ff, tf=tf, mix=mix is not None, final=final_g is not None),
        grid=(m // tm,),
        in_specs=[rows(d)] + mix_specs + [
            pl.BlockSpec((None, 1, d), lambda i: (layer, 0, 0)),
            pl.BlockSpec((None, d, 2 * d_ff), lambda i: (layer, 0, 0), **resident),
            pl.BlockSpec((None, d_ff, d), lambda i: (layer, 0, 0), **resident),
            pl.BlockSpec((1, d), lambda i: (0, 0)),
        ],
        out_specs=rows(d),
        out_shape=jax.ShapeDtypeStruct((m, d), F32),
        scratch_shapes=[pltpu.VMEM((tm, d_ff), BF16)],
        compiler_params=_cparams(1),
        name="ffn",
    )(h2d, *mix_args, norm_g, w13, w2, fg)


def _inproj_kernel(h_ref, g_ref, w_ref, pc_ref, pr_ref, ps_ref):
    xn = _rms_norm_rows(h_ref[...], g_ref[...]).astype(BF16)
    c0 = 2 * D_CONV
    c1 = c0 + D_RWKV_IN
    pc_ref[...] = jnp.dot(xn, w_ref[:, 0:c0], preferred_element_type=F32)
    pr_ref[...] = jnp.dot(xn, w_ref[:, c0:c1], preferred_element_type=F32)
    ps_ref[...] = jnp.dot(xn, w_ref[:, c1:], preferred_element_type=F32).astype(BF16)


def _inproj(h2d, norm_g, w_in, layer, *, tm=512):
    m, d = h2d.shape
    assert m % tm == 0
    d_in = w_in.shape[2]
    widths = (2 * D_CONV, D_RWKV_IN, 3 * D_SB)
    return pl.pallas_call(
        _inproj_kernel,
        grid=(m // tm,),
        in_specs=[
            pl.BlockSpec((tm, d), lambda i: (i, 0)),
            pl.BlockSpec((None, 1, d), lambda i: (layer, 0, 0)),
            pl.BlockSpec((None, d, d_in), lambda i: (layer, 0, 0)),
        ],
        out_specs=[pl.BlockSpec((tm, w), lambda i: (i, 0)) for w in widths],
        out_shape=[jax.ShapeDtypeStruct((m, w), dt) for w, dt in zip(widths, (F32, F32, BF16))],
        compiler_params=_cparams(1),
        name="inproj",
    )(h2d, norm_g, w_in)


CONV_PAD = 32
CONV_SUB = 64


def _conv_kernel(p_ref, w_ref, b_ref, lg_ref, lb_ref, o_ref, buf_ref, *, tt):
    t = pl.program_id(1)

    @pl.when(t == 0)
    def _():
        buf_ref[0:CONV_PAD, :] = jnp.zeros((CONV_PAD, D_CONV), F32)

    val = p_ref[0, :, 0:D_CONV]
    gate = p_ref[0, :, D_CONV:2 * D_CONV]
    buf_ref[CONV_PAD:CONV_PAD + tt, :] = val * _sigmoid(gate)
    first = CONV_PAD - (CONV_WIDTH - 1)
    for s in range(tt // CONV_SUB):
        acc = jnp.zeros((CONV_SUB, D_CONV), F32)
        for j in range(CONV_WIDTH):
            r0 = s * CONV_SUB + first + j
            acc = acc + w_ref[j:j + 1, :] * buf_ref[r0:r0 + CONV_SUB, :]
        y = acc + b_ref[...]
        mu = jnp.mean(y, axis=-1, keepdims=True)
        dlt = y - mu
        var = jnp.mean(dlt * dlt, axis=-1, keepdims=True)
        yn = dlt * lax.rsqrt(var + LN_EPS) * lg_ref[...] + lb_ref[...]
        o_ref[0, s * CONV_SUB:(s + 1) * CONV_SUB, :] = (yn * _sigmoid(yn)).astype(BF16)
    buf_ref[0:CONV_PAD, :] = buf_ref[tt:tt + CONV_PAD, :]


def _conv_group(pc, conv_w, conv_b, ln_g, ln_b, layer, *, tt=128):
    b, lp, _ = pc.shape
    assert lp % tt == 0 and tt % CONV_SUB == 0
    vec = lambda: pl.BlockSpec((None, 1, D_CONV), lambda i, t: (layer, 0, 0))
    return pl.pallas_call(
        functools.partial(_conv_kernel, tt=tt),
        grid=(b, lp // tt),
        in_specs=[
            pl.BlockSpec((1, tt, 2 * D_CONV), lambda i, t: (i, t, 0)),
            pl.BlockSpec((None, CONV_WIDTH, D_CONV), lambda i, t: (layer, 0, 0)),
            vec(), vec(), vec(),
        ],
        out_specs=pl.BlockSpec((1, tt, D_CONV), lambda i, t: (i, t, 0)),
        out_shape=jax.ShapeDtypeStruct((b, lp, D_CONV), BF16),
        scratch_shapes=[pltpu.VMEM((tt + CONV_PAD, D_CONV), F32)],
        compiler_params=_cparams(2),
        name="conv_group",
    )(pc, conv_w, conv_b, ln_g, ln_b)


def _rwkv_kernel(p_ref, mu_ref, w0_ref, wb_ref, a0_ref, ab_ref, gb_ref, kk_ref, ka_ref, rk_ref, lg_ref, lb_ref,
                 bd_ref, tri_ref, o_ref, st_ref, carry_ref, *, nb, tt):
    c = RWKV_CHUNK
    c2 = 2 * c
    n_pairs = D_RWKV // LANES
    chunks_per_seq = tt // c
    n_chunks = nb * chunks_per_seq
    t = pl.program_id(1)

    @pl.when(t == 0)
    def _():
        st_ref[...] = jnp.zeros_like(st_ref)
        carry_ref[...] = jnp.zeros_like(carry_ref)

    x = p_ref[...].reshape(nb * tt, D_RWKV_IN)
    row = lax.broadcasted_iota(jnp.int32, x.shape, 0)
    prev = pltpu.roll(x, 1, 0)
    for e in range(nb):
        prev = jnp.where(row == e * tt, carry_ref[e], prev)
        carry_ref[e] = x[(e + 1) * tt - 1:(e + 1) * tt, :]
    xs = x + mu_ref[...] * (prev - x)

    r = xs[:, 0:D_RWKV]
    k = xs[:, D_RWKV:2 * D_RWKV]
    v = xs[:, 2 * D_RWKV:3 * D_RWKV]
    lora_in = xs[:, 3 * D_RWKV:3 * D_RWKV + DECAY_LORA + AAA_LORA]
    gd = xs[:, 3 * D_RWKV + DECAY_LORA + AAA_LORA:]

    bd = bd_ref[...]

    def head_sum(z):
        hi, lo = _split2(z)
        return jnp.dot(hi, bd, preferred_element_type=F32) + jnp.dot(lo, bd, preferred_element_type=F32)

    w_log = -_softplus(-(w0_ref[...] + _dot(jnp.tanh(lora_in), wb_ref[...]))) - 0.5
    logw = -jnp.exp(w_log)
    a = _sigmoid(a0_ref[...] + _dot(lora_in, ab_ref[...]))
    g = _dot(_sigmoid(gd), gb_ref[...])
    kk = k * kk_ref[...]
    kk = kk / jnp.maximum(jnp.sqrt(head_sum(kk * kk)), 1e-12)
    kq = k * (1.0 + (a - 1.0) * ka_ref[...])
    na = -kk
    b = kk * a
    bonus = head_sum(r * kq * rk_ref[...]) * v

    tri = tri_ref[...]
    l1, l2, l3 = _split3(logw)
    cl = (jnp.dot(tri, l1, preferred_element_type=F32) + jnp.dot(tri, l2, preferred_element_type=F32)
          + jnp.dot(tri, l3, preferred_element_type=F32))
    cl_ends = [cl[(s + 1) * c - 1:(s + 1) * c, :] for s in range(n_chunks)]
    cl_end = jnp.concatenate([jnp.broadcast_to(e, (c, D_RWKV)) for e in cl_ends], axis=0)
    e_pos = jnp.exp(cl)
    e_neg = jnp.exp(-cl)
    e_end = jnp.exp(cl_end - cl)
    at_all = jnp.exp(cl - logw) * na
    bt_all = b * e_neg
    kt_all = kq * e_neg
    rt_all = r * e_pos
    bp_all = b * e_end
    kp_all = kq * e_end

    lane = lax.broadcasted_iota(jnp.int32, (c, LANES), 1)
    lane2 = lax.broadcasted_iota(jnp.int32, (c2, LANES), 1)
    m0 = lane < HEAD_DIM
    m0_2 = lane2 < HEAD_DIM
    ri = lax.broadcasted_iota(jnp.int32, (c2, c2), 0)
    cj = lax.broadcasted_iota(jnp.int32, (c2, c2), 1)
    ti = jnp.bitwise_and(ri, c - 1)
    sj = jnp.bitwise_and(cj, c - 1)
    mask_z = (ti > sj) | ((ri >= c) & (ti == sj))
    left_top = lax.broadcasted_iota(jnp.int32, (c, c2), 1) < c
    blockdiag = jnp.bitwise_and(ri, c) == jnp.bitwise_and(cj, c)
    eye = (ri == cj)
    zero = jnp.zeros((c, LANES), F32)
    rows = lambda parts: jnp.concatenate(parts, axis=0)
    cols = lambda parts: jnp.concatenate(parts, axis=1)

    units = [(s, j) for s in range(n_chunks) for j in range(n_pairs)]

    def tile(arr, u):
        s, j = u
        return arr[s * c:(s + 1) * c, j * LANES:(j + 1) * LANES]

    at = [tile(at_all, u) for u in units]
    rt = [tile(rt_all, u) for u in units]
    vv = [tile(v, u) for u in units]

    z0, z1 = [], []
    for i, u in enumerate(units):
        xx = rows([at[i], rt[i]])
        bt, kt = tile(bt_all, u), tile(kt_all, u)
        z0.append(jnp.where(mask_z, _dot_nt(jnp.where(m0_2, xx, 0.0), rows([bt, kt])), 0.0))
        z1.append(jnp.where(mask_z, _dot_nt(jnp.where(m0_2, 0.0, xx), rows([kt, bt])), 0.0))

    pw = [rows([jnp.where(left_top, z0[i][0:c], 0.0), jnp.where(left_top, 0.0, z1[i][0:c])]) for i in range(len(units))]
    tinv = [jnp.where(eye, 1.0, 0.0) + p for p in pw]
    for _ in range(5):
        pw = [_dot(p, p) for p in pw]
        tinv = [ti_ + _dot(ti_, p) for ti_, p in zip(tinv, pw)]

    akv = []
    for i in range(len(units)):
        v2 = rows([vv[i], vv[i]])
        akv.append(jnp.where(m0, _dot(jnp.where(left_top, 0.0, z0[i][0:c]), v2),
                             _dot(jnp.where(left_top, z1[i][0:c], 0.0), v2)))

    ah, uv = [], []
    for i in range(len(units)):
        wa = rows([jnp.where(m0, at[i], 0.0), jnp.where(m0, 0.0, at[i])])
        wv = rows([jnp.where(m0, akv[i], 0.0), jnp.where(m0, 0.0, akv[i])])
        tw = _dot(tinv[i], cols([wa, wv]))
        ah.append(tw[0:c, 0:LANES] + tw[c:c2, 0:LANES])
        uv.append(tw[0:c, LANES:] + tw[c:c2, LANES:])

    rh, ov, gmat, hmat = [], [], [], []
    for i, u in enumerate(units):
        rhs0 = cols([rows([ah[i], zero]), rows([uv[i], vv[i]])])
        rhs1 = cols([rows([zero, ah[i]]), rows([vv[i], uv[i]])])
        res0 = _dot(z0[i][c:c2], rhs0)
        res1 = _dot(z1[i][c:c2], rhs1)
        rh.append(rt[i] + jnp.where(m0, res0[:, 0:LANES], res1[:, 0:LANES]))
        ov.append(jnp.where(m0, res0[:, LANES:], res1[:, LANES:]))
        ypt = rows([tile(bp_all, u), tile(kp_all, u)]).T
        gh = _dot(ypt, rhs0)
        s, j = u
        p_end = jnp.exp(cl_ends[s][:, j * LANES:(j + 1) * LANES])
        gmat.append(jnp.where(blockdiag, gh[:, 0:LANES], 0.0)
                    + jnp.where(eye, jnp.broadcast_to(p_end, (c2, LANES)), 0.0))
        hmat.append(jnp.where(blockdiag, gh[:, LANES:], 0.0))

    states = [st_ref[n] for n in range(nb * n_pairs)]
    out_rows = []
    for s in range(n_chunks):
        out_pairs = []
        for j in range(n_pairs):
            i = s * n_pairs + j
            n = (s // chunks_per_seq) * n_pairs + j
            out_pairs.append(_dot(rh[i], states[n]) + ov[i])
            states[n] = _dot(gmat[i], states[n]) + hmat[i]
        out_rows.append(cols(out_pairs))
    for n in range(nb * n_pairs):
        st_ref[n] = states[n]

    o = rows(out_rows)
    mean = head_sum(o) * (1.0 / HEAD_DIM)
    dlt = o - mean
    var = head_sum(dlt * dlt) * (1.0 / HEAD_DIM)
    on = dlt * lax.rsqrt(var + GN_EPS) * lg_ref[...] + lb_ref[...]
    o_ref[...] = ((on + bonus) * g).astype(BF16).reshape(nb, tt, D_RWKV)


def _rwkv_group(pr, mu, w0, wb_ext, a0, ab_ext, gb, k_k, k_a, r_k, ln_g, ln_b, bd, layer, *, nb=2, tt=128):
    b, lp, _ = pr.shape
    assert b % nb == 0 and lp % tt == 0 and tt % RWKV_CHUNK == 0
    n_pairs = D_RWKV // LANES
    idx = jnp.arange(nb * tt)
    tri = ((idx[:, None] >= idx[None, :]) & (idx[:, None] // RWKV_CHUNK == idx[None, :] // RWKV_CHUNK)).astype(BF16)

    def par(arr):
        return pl.BlockSpec((None,) + arr.shape[1:], lambda i, t: (layer,) + (0,) * (arr.ndim - 1))

    def const(arr):
        return pl.BlockSpec(arr.shape, lambda i, t: (0,) * arr.ndim)

    params = (mu, w0, wb_ext, a0, ab_ext, gb, k_k, k_a, r_k, ln_g, ln_b)
    return pl.pallas_call(
        functools.partial(_rwkv_kernel, nb=nb, tt=tt),
        grid=(b // nb, lp // tt),
        in_specs=[pl.BlockSpec((nb, tt, D_RWKV_IN), lambda i, t: (i, t, 0))] + [par(p) for p in params]
        + [const(bd), const(tri)],
        out_specs=pl.BlockSpec((nb, tt, D_RWKV), lambda i, t: (i, t, 0)),
        out_shape=jax.ShapeDtypeStruct((b, lp, D_RWKV), BF16),
        scratch_shapes=[pltpu.VMEM((nb * n_pairs, LANES, LANES), F32), pltpu.VMEM((nb, 1, D_RWKV_IN), F32)],
        compiler_params=_cparams(2),
        name="rwkv_group",
    )(pr, *params, bd, tri)


SB_TILE = 128


def _sb_kernel(q_ref, k_ref, v_ref, g_ref, bd_ref, later_ref, o_ref, km_ref, vm_ref):
    t = SB_TILE
    n_pairs = D_SB // LANES
    qi = pl.program_id(1)

    @pl.when(qi == 0)
    def _():
        m0 = lax.broadcasted_iota(jnp.int32, (k_ref.shape[1], LANES), 1) < HEAD_DIM
        for j in range(n_pairs):
            kj = k_ref[0, :, j * LANES:(j + 1) * LANES]
            vj = v_ref[0, :, j * LANES:(j + 1) * LANES]
            zero = jnp.zeros_like(kj)
            km_ref[2 * j] = jnp.where(m0, kj, zero)
            km_ref[2 * j + 1] = jnp.where(m0, zero, kj)
            vm_ref[2 * j] = jnp.where(m0, vj, zero)
            vm_ref[2 * j + 1] = jnp.where(m0, zero, vj)

    rowi = lax.broadcasted_iota(jnp.int32, (t, t), 0)
    coli = lax.broadcasted_iota(jnp.int32, (t, t), 1)
    causal = coli < rowi
    causal2 = jnp.concatenate([causal, causal], axis=1)
    later2 = later_ref[...]
    qb = [q_ref[0, :, j * LANES:(j + 1) * LANES] * (HEAD_DIM ** -0.5) for j in range(n_pairs)]

    def both_heads(ref, kb, j):
        off = kb * t if isinstance(kb, int) else pl.multiple_of(kb * t, t)
        return jnp.concatenate([ref[2 * j, pl.ds(off, t), :], ref[2 * j + 1, pl.ds(off, t), :]], axis=0)

    def scores(kbs):
        return tuple(lax.dot_general(qb[j], both_heads(km_ref, kb, j), (((1,), (1,)), ((), ())),
                                     preferred_element_type=F32) for kb in kbs for j in range(n_pairs))

    def group(kbs, zs, accs, carries, diagonal):
        units = [(kb, j) for kb in kbs for j in range(n_pairs)]
        sps = [_softplus(z) for z in zs]
        if diagonal:
            sps = [jnp.where(causal2, sp, 0.0) for sp in sps]
        locs = [jnp.dot(sp.astype(BF16), later2, preferred_element_type=F32) for sp in sps]
        accs, carries = list(accs), list(carries)
        for i, (kb, j) in enumerate(units):
            halves = []
            for m in range(2):
                sl = slice(m * t, (m + 1) * t)
                att = jnp.exp(zs[i][:, sl] - locs[i][:, sl] - carries[2 * j + m])
                if diagonal:
                    att = jnp.where(causal, att, 0.0)
                halves.append(att.astype(BF16))
                carries[2 * j + m] = carries[2 * j + m] + jnp.sum(sps[i][:, sl], axis=1, keepdims=True)
            accs[j] = accs[j] + jnp.dot(jnp.concatenate(halves, axis=1), both_heads(vm_ref, kb, j),
                                        preferred_element_type=F32)
        return tuple(accs), tuple(carries)

    state = (tuple(jnp.zeros((t, LANES), F32) for _ in range(n_pairs)),
             tuple(jnp.zeros((t, 1), F32) for _ in range(2 * n_pairs)))
    state = group([qi], scores([qi]), *state, True)

    def run(first, n_iters, width, state):
        def body(it, c):
            kbs = [first - width * it - w for w in range(width)]
            zs = scores(kbs)
            for w0 in range(0, width, 2):
                c = group(kbs[w0:w0 + 2], zs[w0 * n_pairs:(w0 + 2) * n_pairs], *c, False)
            return c
        return lax.fori_loop(0, n_iters, body, state)

    n4 = jnp.right_shift(qi, 2)
    n2 = jnp.bitwise_and(jnp.right_shift(qi, 1), 1)
    state = run(qi - 1, n4, 4, state)
    state = run(qi - 1 - 4 * n4, n2, 2, state)
    state = run(0, jnp.bitwise_and(qi, 1), 1, state)
    accs = state[0]

    for j in range(n_pairs):
        o = accs[j]
        hi, lo = _split2(o * o)
        ss = jnp.dot(hi, bd_ref[...], preferred_element_type=F32) + jnp.dot(lo, bd_ref[...], preferred_element_type=F32)
        o_ref[0, :, j * LANES:(j + 1) * LANES] = (o * lax.rsqrt(ss * (1.0 / HEAD_DIM) + RMS_EPS)
                                                  * g_ref[:, j * LANES:(j + 1) * LANES]).astype(BF16)


def _sb_group(ps, norm_g, bd128, layer):
    b, lp, _ = ps.shape
    assert lp % SB_TILE == 0
    n_pairs = D_SB // LANES
    idx = jnp.arange(2 * SB_TILE)
    later2 = ((idx[:, None] >= idx[None, :]) & (idx[:, None] // SB_TILE == idx[None, :] // SB_TILE)).astype(BF16)
    return pl.pallas_call(
        _sb_kernel,
        grid=(b, lp // SB_TILE),
        in_specs=[
            pl.BlockSpec((1, SB_TILE, D_SB), lambda i, q: (i, q, 0)),
            pl.BlockSpec((1, lp, D_SB), lambda i, q: (i, 0, 1)),
            pl.BlockSpec((1, lp, D_SB), lambda i, q: (i, 0, 2)),
            pl.BlockSpec((None, 1, D_SB), lambda i, q: (layer, 0, 0)),
            pl.BlockSpec((LANES, LANES), lambda i, q: (0, 0)),
            pl.BlockSpec(later2.shape, lambda i, q: (0, 0)),
        ],
        out_specs=pl.BlockSpec((1, SB_TILE, D_SB), lambda i, q: (i, q, 0)),
        out_shape=jax.ShapeDtypeStruct((b, lp, D_SB), BF16),
        scratch_shapes=[pltpu.VMEM((2 * n_pairs, lp, LANES), BF16), pltpu.VMEM((2 * n_pairs, lp, LANES), BF16)],
        compiler_params=_cparams(2),
        name="sb_group",
    )(ps, ps, ps, norm_g, bd128, later2)


def _block_diag_ones(n):
    i = jnp.arange(n) // HEAD_DIM
    return (i[:, None] == i[None, :]).astype(BF16)


@jax.jit
def _trunk(x, meta, ffn1_norm, ffn1_w13, ffn1_w2, mix_norm, w_in, conv_w, conv_b, conv_ln_g, conv_ln_b, rwkv_mu,
           rwkv_w0, rwkv_wB, rwkv_a0, rwkv_aB, rwkv_gB, rwkv_kk, rwkv_ka, rwkv_rk, rwkv_ln_g, rwkv_ln_b, sb_norm,
           w_out, ffn2_norm, ffn2_w13, ffn2_w2, final_norm):
    bsz, seq, d = x.shape
    depth = w_in.shape[0]
    l_real = N_META + seq
    lp = -(-l_real // SEQ_ALIGN) * SEQ_ALIGN
    meta_b = jnp.broadcast_to(meta.astype(x.dtype)[None], (bsz, N_META, d))
    h = jnp.concatenate([meta_b, x, jnp.zeros((bsz, lp - l_real, d), x.dtype)], axis=1).reshape(bsz * lp, d)

    row3 = lambda p: p.reshape(depth, 1, -1)
    ffn1_w13, ffn1_w2, ffn2_w13, ffn2_w2 = (w.astype(BF16) for w in (ffn1_w13, ffn1_w2, ffn2_w13, ffn2_w2))
    w_in, w_out = w_in.astype(BF16), w_out.astype(BF16)
    ffn1_norm, mix_norm, ffn2_norm, sb_norm = row3(ffn1_norm), row3(mix_norm), row3(ffn2_norm), row3(sb_norm)
    conv_b, conv_ln_g, conv_ln_b = row3(conv_b), row3(conv_ln_g), row3(conv_ln_b)
    rwkv_vecs = [row3(p) for p in (rwkv_mu, rwkv_w0, rwkv_a0, rwkv_kk, rwkv_ka, rwkv_rk, rwkv_ln_g, rwkv_ln_b)]
    mu, w0, a0, k_k, k_a, r_k, ln_g, ln_b = rwkv_vecs
    wb_ext = jnp.pad(rwkv_wB, ((0, 0), (0, AAA_LORA), (0, 0))).astype(BF16)
    ab_ext = jnp.pad(rwkv_aB, ((0, 0), (DECAY_LORA, 0), (0, 0))).astype(BF16)
    gb = rwkv_gB.astype(BF16)
    bd_rwkv = _block_diag_ones(D_RWKV)
    bd_pair = _block_diag_ones(LANES)

    for l in range(depth):
        h = _ffn(h, ffn1_norm, ffn1_w13, ffn1_w2, l)
        pc, pr, ps = _inproj(h, mix_norm, w_in, l)
        yc = _conv_group(pc.reshape(bsz, lp, -1), conv_w, conv_b, conv_ln_g, conv_ln_b, l)
        yr = _rwkv_group(pr.reshape(bsz, lp, -1), mu, w0, wb_ext, a0, ab_ext, gb, k_k, k_a, r_k, ln_g, ln_b,
                         bd_rwkv, l)
        ys = _sb_group(ps.reshape(bsz, lp, -1), sb_norm, bd_pair, l)
        mix = (yc.reshape(bsz * lp, -1), yr.reshape(bsz * lp, -1), ys.reshape(bsz * lp, -1), w_out)
        h = _ffn(h, ffn2_norm, ffn2_w13, ffn2_w2, l, mix=mix,
                 final_g=final_norm.reshape(1, d) if l == depth - 1 else None)
    out = h
    return out.reshape(bsz, lp, d)[:, N_META:l_real]


def kernel(x, meta, ffn1_norm, ffn1_w13, ffn1_w2, mix_norm, w_in, conv_w, conv_b, conv_ln_g, conv_ln_b, rwkv_mu,
           rwkv_w0, rwkv_wB, rwkv_a0, rwkv_aB, rwkv_gB, rwkv_kk, rwkv_ka, rwkv_rk, rwkv_ln_g, rwkv_ln_b, sb_norm,
           w_out, ffn2_norm, ffn2_w13, ffn2_w2, final_norm):
    return _trunk(x, meta, ffn1_norm, ffn1_w13, ffn1_w2, mix_norm, w_in, conv_w, conv_b, conv_ln_g, conv_ln_b,
                  rwkv_mu, rwkv_w0, rwkv_wB, rwkv_a0, rwkv_aB, rwkv_gB, rwkv_kk, rwkv_ka, rwkv_rk, rwkv_ln_g,
                  rwkv_ln_b, sb_norm, w_out, ffn2_norm, ffn2_w13, ffn2_w2, final_norm)
```

```python
import functools

import jax
import jax.numpy as jnp
from jax import lax
from jax.experimental import pallas as pl
from jax.experimental.pallas import tpu as pltpu

F32 = jnp.float32
BF16 = jnp.bfloat16

N_META = 16
HEAD_DIM = 64
D_CONV = 256
CONV_WIDTH = 31
D_RWKV = 384
D_SB = 384
DECAY_LORA = 64
AAA_LORA = 64
GATE_LORA = 128
D_RWKV_IN = 3 * D_RWKV + DECAY_LORA + AAA_LORA + GATE_LORA
RMS_EPS = 1e-6
LN_EPS = 1e-5
GN_EPS = 64e-5

LANES = 128
SEQ_ALIGN = 128
RWKV_CHUNK = 64
VMEM_LIMIT = 56 * 1024 * 1024


def _cparams(n_axes):
    return pltpu.CompilerParams(dimension_semantics=("arbitrary",) * n_axes, vmem_limit_bytes=VMEM_LIMIT)


def _dot(a, b):
    return jnp.dot(a.astype(BF16), b.astype(BF16), preferred_element_type=F32)


def _dot_nt(a, b):
    return lax.dot_general(a.astype(BF16), b.astype(BF16), (((1,), (1,)), ((), ())), preferred_element_type=F32)


def _split3(x):
    h1 = x.astype(BF16)
    r1 = x - h1.astype(F32)
    h2 = r1.astype(BF16)
    return h1, h2, (r1 - h2.astype(F32)).astype(BF16)


def _split2(x):
    hi = x.astype(BF16)
    return hi, (x - hi.astype(F32)).astype(BF16)


def _sigmoid(x):
    return 1.0 / (1.0 + jnp.exp(-x))


def _softplus(x):
    return jnp.maximum(x, 0.0) + jnp.log(1.0 + jnp.exp(-jnp.abs(x)))


def _rms_norm_rows(x, g):
    ms = jnp.mean(x * x, axis=-1, keepdims=True)
    return x * lax.rsqrt(ms + RMS_EPS) * g


def _ffn_kernel(h_ref, yc_ref, yr_ref, ys_ref, wo_ref, g_ref, w13_ref, w2_ref, fg_ref, o_ref, act_ref, *,
                d_ff, tf, mix, final):
    hin = h_ref[...]
    if mix:
        r0 = D_CONV
        r1 = D_CONV + D_RWKV
        hin = hin + jnp.dot(yc_ref[...], wo_ref[0:r0, :], preferred_element_type=F32)
        hin = hin + jnp.dot(yr_ref[...], wo_ref[r0:r1, :], preferred_element_type=F32)
        hin = hin + jnp.dot(ys_ref[...], wo_ref[r1:, :], preferred_element_type=F32)
    o_ref[...] = hin
    xn = _rms_norm_rows(hin, g_ref[...]).astype(BF16)
    for c0 in range(0, d_ff, tf):
        gate = jnp.dot(xn, w13_ref[:, c0:c0 + tf], preferred_element_type=F32)
        up = jnp.dot(xn, w13_ref[:, d_ff + c0:d_ff + c0 + tf], preferred_element_type=F32)
        act_ref[:, c0:c0 + tf] = (gate * _sigmoid(gate) * up).astype(BF16)
    out = o_ref[...] + 0.5 * jnp.dot(act_ref[...], w2_ref[...], preferred_element_type=F32)
    if final:
        out = _rms_norm_rows(out, fg_ref[...])
    o_ref[...] = out


def _ffn(h2d, norm_g, w13, w2, layer, *, mix=None, final_g=None, tm=512, tf=256):
    m, d = h2d.shape
    d_ff = w2.shape[1]
    assert m % tm == 0 and d_ff % tf == 0
    resident = dict(pipeline_mode=pl.Buffered(1))
    rows = lambda w: pl.BlockSpec((tm, w), lambda i: (i, 0))
    if mix is None:
        mix_args = [jnp.zeros((tm, w), BF16) for w in (D_CONV, D_RWKV, D_SB)] + [jnp.zeros((1, 16, d), BF16)]
        mix_specs = [pl.BlockSpec((tm, w), lambda i: (0, 0)) for w in (D_CONV, D_RWKV, D_SB)]
        wo_layer = 0
    else:
        mix_args = list(mix)
        mix_specs = [rows(D_CONV), rows(D_RWKV), rows(D_SB)]
        wo_layer = layer
    mix_specs.append(pl.BlockSpec((None,) + mix_args[3].shape[1:], lambda i: (wo_layer, 0, 0), **resident))
    fg = jnp.zeros((1, d), F32) if final_g is None else final_g
    return pl.pallas_call(
        functools.partial(_ffn_kernel, d_ff=d_ff, tf=tf, mix=mix is not None, final=final_g is not None),
        grid=(m // tm,),
        in_specs=[rows(d)] + mix_specs + [
            pl.BlockSpec((None, 1, d), lambda i: (layer, 0, 0)),
            pl.BlockSpec((None, d, 2 * d_ff), lambda i: (layer, 0, 0), **resident),
            pl.BlockSpec((None, d_ff, d), lambda i: (layer, 0, 0), **resident),
            pl.BlockSpec((1, d), lambda i: (0, 0)),
        ],
        out_specs=rows(d),
        out_shape=jax.ShapeDtypeStruct((m, d), F32),
        scratch_shapes=[pltpu.VMEM((tm, d_ff), BF16)],
        compiler_params=_cparams(1),
        name="ffn",
    )(h2d, *mix_args, norm_g, w13, w2, fg)


def _inproj_kernel(h_ref, g_ref, w_ref, pc_ref, pr_ref, ps_ref):
    xn = _rms_norm_rows(h_ref[...], g_ref[...]).astype(BF16)
    c0 = 2 * D_CONV
    c1 = c0 + D_RWKV_IN
    pc_ref[...] = jnp.dot(xn, w_ref[:, 0:c0], preferred_element_type=F32)
    pr_ref[...] = jnp.dot(xn, w_ref[:, c0:c1], preferred_element_type=F32)
    ps_ref[...] = jnp.dot(xn, w_ref[:, c1:], preferred_element_type=F32).astype(BF16)


def _inproj(h2d, norm_g, w_in, layer, *, tm=512):
    m, d = h2d.shape
    assert m % tm == 0
    d_in = w_in.shape[2]
    widths = (2 * D_CONV, D_RWKV_IN, 3 * D_SB)
    return pl.pallas_call(
        _inproj_kernel,
        grid=(m // tm,),
        in_specs=[
            pl.BlockSpec((tm, d), lambda i: (i, 0)),
            pl.BlockSpec((None, 1, d), lambda i: (layer, 0, 0)),
            pl.BlockSpec((None, d, d_in), lambda i: (layer, 0, 0)),
        ],
        out_specs=[pl.BlockSpec((tm, w), lambda i: (i, 0)) for w in widths],
        out_shape=[jax.ShapeDtypeStruct((m, w), dt) for w, dt in zip(widths, (F32, F32, BF16))],
        compiler_params=_cparams(1),
        name="inproj",
    )(h2d, norm_g, w_in)


CONV_PAD = 32
CONV_SUB = 64


def _conv_kernel(p_ref, w_ref, b_ref, lg_ref, lb_ref, o_ref, buf_ref, *, tt):
    t = pl.program_id(1)

    @pl.when(t == 0)
    def _():
        buf_ref[0:CONV_PAD, :] = jnp.zeros((CONV_PAD, D_CONV), F32)

    val = p_ref[0, :, 0:D_CONV]
    gate = p_ref[0, :, D_CONV:2 * D_CONV]
    buf_ref[CONV_PAD:CONV_PAD + tt, :] = val * _sigmoid(gate)
    first = CONV_PAD - (CONV_WIDTH - 1)
    for s in range(tt // CONV_SUB):
        acc = jnp.zeros((CONV_SUB, D_CONV), F32)
        for j in range(CONV_WIDTH):
            r0 = s * CONV_SUB + first + j
            acc = acc + w_ref[j:j + 1, :] * buf_ref[r0:r0 + CONV_SUB, :]
        y = acc + b_ref[...]
        mu = jnp.mean(y, axis=-1, keepdims=True)
        dlt = y - mu
        var = jnp.mean(dlt * dlt, axis=-1, keepdims=True)
        yn = dlt * lax.rsqrt(var + LN_EPS) * lg_ref[...] + lb_ref[...]
        o_ref[0, s * CONV_SUB:(s + 1) * CONV_SUB, :] = (yn * _sigmoid(yn)).astype(BF16)
    buf_ref[0:CONV_PAD, :] = buf_ref[tt:tt + CONV_PAD, :]


def _conv_group(pc, conv_w, conv_b, ln_g, ln_b, layer, *, tt=128):
    b, lp, _ = pc.shape
    assert lp % tt == 0 and tt % CONV_SUB == 0
    vec = lambda: pl.BlockSpec((None, 1, D_CONV), lambda i, t: (layer, 0, 0))
    return pl.pallas_call(
        functools.partial(_conv_kernel, tt=tt),
        grid=(b, lp // tt),
        in_specs=[
            pl.BlockSpec((1, tt, 2 * D_CONV), lambda i, t: (i, t, 0)),
            pl.BlockSpec((None, CONV_WIDTH, D_CONV), lambda i, t: (layer, 0, 0)),
            vec(), vec(), vec(),
        ],
        out_specs=pl.BlockSpec((1, tt, D_CONV), lambda i, t: (i, t, 0)),
        out_shape=jax.ShapeDtypeStruct((b, lp, D_CONV), BF16),
        scratch_shapes=[pltpu.VMEM((tt + CONV_PAD, D_CONV), F32)],
        compiler_params=_cparams(2),
        name="conv_group",
    )(pc, conv_w, conv_b, ln_g, ln_b)


def _rwkv_kernel(p_ref, mu_ref, w0_ref, wb_ref, a0_ref, ab_ref, gb_ref, kk_ref, ka_ref, rk_ref, lg_ref, lb_ref,
                 bd_ref, tri_ref, o_ref, st_ref, carry_ref, *, nb, tt):
    c = RWKV_CHUNK
    c2 = 2 * c
    n_pairs = D_RWKV // LANES
    chunks_per_seq = tt // c
    n_chunks = nb * chunks_per_seq
    t = pl.program_id(1)

    @pl.when(t == 0)
    def _():
        st_ref[...] = jnp.zeros_like(st_ref)
        carry_ref[...] = jnp.zeros_like(carry_ref)

    x = p_ref[...].reshape(nb * tt, D_RWKV_IN)
    row = lax.broadcasted_iota(jnp.int32, x.shape, 0)
    prev = pltpu.roll(x, 1, 0)
    for e in range(nb):
        prev = jnp.where(row == e * tt, carry_ref[e], prev)
        carry_ref[e] = x[(e + 1) * tt - 1:(e + 1) * tt, :]
    xs = x + mu_ref[...] * (prev - x)

    r = xs[:, 0:D_RWKV]
    k = xs[:, D_RWKV:2 * D_RWKV]
    v = xs[:, 2 * D_RWKV:3 * D_RWKV]
    lora_in = xs[:, 3 * D_RWKV:3 * D_RWKV + DECAY_LORA + AAA_LORA]
    gd = xs[:, 3 * D_RWKV + DECAY_LORA + AAA_LORA:]

    bd = bd_ref[...]

    def head_sum(z):
        hi, lo = _split2(z)
        return jnp.dot(hi, bd, preferred_element_type=F32) + jnp.dot(lo, bd, preferred_element_type=F32)

    w_log = -_softplus(-(w0_ref[...] + _dot(jnp.tanh(lora_in), wb_ref[...]))) - 0.5
    logw = -jnp.exp(w_log)
    a = _sigmoid(a0_ref[...] + _dot(lora_in, ab_ref[...]))
    g = _dot(_sigmoid(gd), gb_ref[...])
    kk = k * kk_ref[...]
    kk = kk / jnp.maximum(jnp.sqrt(head_sum(kk * kk)), 1e-12)
    kq = k * (1.0 + (a - 1.0) * ka_ref[...])
    na = -kk
    b = kk * a
    bonus = head_sum(r * kq * rk_ref[...]) * v

    tri = tri_ref[...]
    l1, l2, l3 = _split3(logw)
    cl = (jnp.dot(tri, l1, preferred_element_type=F32) + jnp.dot(tri, l2, preferred_element_type=F32)
          + jnp.dot(tri, l3, preferred_element_type=F32))
    cl_ends = [cl[(s + 1) * c - 1:(s + 1) * c, :] for s in range(n_chunks)]
    cl_end = jnp.concatenate([jnp.broadcast_to(e, (c, D_RWKV)) for e in cl_ends], axis=0)
    e_pos = jnp.exp(cl)
    e_neg = jnp.exp(-cl)
    e_end = jnp.exp(cl_end - cl)
    at_all = jnp.exp(cl - logw) * na
    bt_all = b * e_neg
    kt_all = kq * e_neg
    rt_all = r * e_pos
    bp_all = b * e_end
    kp_all = kq * e_end

    lane = lax.broadcasted_iota(jnp.int32, (c, LANES), 1)
    lane2 = lax.broadcasted_iota(jnp.int32, (c2, LANES), 1)
    m0 = lane < HEAD_DIM
    m0_2 = lane2 < HEAD_DIM
    ri = lax.broadcasted_iota(jnp.int32, (c2, c2), 0)
    cj = lax.broadcasted_iota(jnp.int32, (c2, c2), 1)
    ti = jnp.bitwise_and(ri, c - 1)
    sj = jnp.bitwise_and(cj, c - 1)
    mask_z = (ti > sj) | ((ri >= c) & (ti == sj))
    left_top = lax.broadcasted_iota(jnp.int32, (c, c2), 1) < c
    blockdiag = jnp.bitwise_and(ri, c) == jnp.bitwise_and(cj, c)
    eye = (ri == cj)
    zero = jnp.zeros((c, LANES), F32)
    rows = lambda parts: jnp.concatenate(parts, axis=0)
    cols = lambda parts: jnp.concatenate(parts, axis=1)

    units = [(s, j) for s in range(n_chunks) for j in range(n_pairs)]

    def tile(arr, u):
        s, j = u
        return arr[s * c:(s + 1) * c, j * LANES:(j + 1) * LANES]

    at = [tile(at_all, u) for u in units]
    rt = [tile(rt_all, u) for u in units]
    vv = [tile(v, u) for u in units]

    z0, z1 = [], []
    for i, u in enumerate(units):
        xx = rows([at[i], rt[i]])
        bt, kt = tile(bt_all, u), tile(kt_all, u)
        z0.append(jnp.where(mask_z, _dot_nt(jnp.where(m0_2, xx, 0.0), rows([bt, kt])), 0.0))
        z1.append(jnp.where(mask_z, _dot_nt(jnp.where(m0_2, 0.0, xx), rows([kt, bt])), 0.0))

    pw = [rows([jnp.where(left_top, z0[i][0:c], 0.0), jnp.where(left_top, 0.0, z1[i][0:c])]) for i in range(len(units))]
    tinv = [jnp.where(eye, 1.0, 0.0) + p for p in pw]
    for _ in range(5):
        pw = [_dot(p, p) for p in pw]
        tinv = [ti_ + _dot(ti_, p) for ti_, p in zip(tinv, pw)]

    akv = []
    for i in range(len(units)):
        v2 = rows([vv[i], vv[i]])
        akv.append(jnp.where(m0, _dot(jnp.where(left_top, 0.0, z0[i][0:c]), v2),
                             _dot(jnp.where(left_top, z1[i][0:c], 0.0), v2)))

    ah, uv = [], []
    for i in range(len(units)):
        wa = rows([jnp.where(m0, at[i], 0.0), jnp.where(m0, 0.0, at[i])])
        wv = rows([jnp.where(m0, akv[i], 0.0), jnp.where(m0, 0.0, akv[i])])
        tw = _dot(tinv[i], cols([wa, wv]))
        ah.append(tw[0:c, 0:LANES] + tw[c:c2, 0:LANES])
        uv.append(tw[0:c, LANES:] + tw[c:c2, LANES:])

    rh, ov, gmat, hmat = [], [], [], []
    for i, u in enumerate(units):
        rhs0 = cols([rows([ah[i], zero]), rows([uv[i], vv[i]])])
        rhs1 = cols([rows([zero, ah[i]]), rows([vv[i], uv[i]])])
        res0 = _dot(z0[i][c:c2], rhs0)
        res1 = _dot(z1[i][c:c2], rhs1)
        rh.append(rt[i] + jnp.where(m0, res0[:, 0:LANES], res1[:, 0:LANES]))
        ov.append(jnp.where(m0, res0[:, LANES:], res1[:, LANES:]))
        ypt = rows([tile(bp_all, u), tile(kp_all, u)]).T
        gh = _dot(ypt, rhs0)
        s, j = u
        p_end = jnp.exp(cl_ends[s][:, j * LANES:(j + 1) * LANES])
        gmat.append(jnp.where(blockdiag, gh[:, 0:LANES], 0.0)
                    + jnp.where(eye, jnp.broadcast_to(p_end, (c2, LANES)), 0.0))
        hmat.append(jnp.where(blockdiag, gh[:, LANES:], 0.0))

    states = [st_ref[n] for n in range(nb * n_pairs)]
    out_rows = []
    for s in range(n_chunks):
        out_pairs = []
        for j in range(n_pairs):
            i = s * n_pairs + j
            n = (s // chunks_per_seq) * n_pairs + j
            out_pairs.append(_dot(rh[i], states[n]) + ov[i])
            states[n] = _dot(gmat[i], states[n]) + hmat[i]
        out_rows.append(cols(out_pairs))
    for n in range(nb * n_pairs):
        st_ref[n] = states[n]

    o = rows(out_rows)
    mean = head_sum(o) * (1.0 / HEAD_DIM)
    dlt = o - mean
    var = head_sum(dlt * dlt) * (1.0 / HEAD_DIM)
    on = dlt * lax.rsqrt(var + GN_EPS) * lg_ref[...] + lb_ref[...]
    o_ref[...] = ((on + bonus) * g).astype(BF16).reshape(nb, tt, D_RWKV)


def _rwkv_group(pr, mu, w0, wb_ext, a0, ab_ext, gb, k_k, k_a, r_k, ln_g, ln_b, bd, layer, *, nb=2, tt=128):
    b, lp, _ = pr.shape
    assert b % nb == 0 and lp % tt == 0 and tt % RWKV_CHUNK == 0
    n_pairs = D_RWKV // LANES
    idx = jnp.arange(nb * tt)
    tri = ((idx[:, None] >= idx[None, :]) & (idx[:, None] // RWKV_CHUNK == idx[None, :] // RWKV_CHUNK)).astype(BF16)

    def par(arr):
        return pl.BlockSpec((None,) + arr.shape[1:], lambda i, t: (layer,) + (0,) * (arr.ndim - 1))

    def const(arr):
        return pl.BlockSpec(arr.shape, lambda i, t: (0,) * arr.ndim)

    params = (mu, w0, wb_ext, a0, ab_ext, gb, k_k, k_a, r_k, ln_g, ln_b)
    return pl.pallas_call(
        functools.partial(_rwkv_kernel, nb=nb, tt=tt),
        grid=(b // nb, lp // tt),
        in_specs=[pl.BlockSpec((nb, tt, D_RWKV_IN), lambda i, t: (i, t, 0))] + [par(p) for p in params]
        + [const(bd), const(tri)],
        out_specs=pl.BlockSpec((nb, tt, D_RWKV), lambda i, t: (i, t, 0)),
        out_shape=jax.ShapeDtypeStruct((b, lp, D_RWKV), BF16),
        scratch_shapes=[pltpu.VMEM((nb * n_pairs, LANES, LANES), F32), pltpu.VMEM((nb, 1, D_RWKV_IN), F32)],
        compiler_params=_cparams(2),
        name="rwkv_group",
    )(pr, *params, bd, tri)


SB_TILE = 128


def _sb_kernel(q_ref, k_ref, v_ref, g_ref, bd_ref, later_ref, o_ref, km_ref, vm_ref, *, nb):
    t = SB_TILE
    n_pairs = D_SB // LANES
    n_streams = nb * n_pairs
    qi = pl.program_id(1)

    @pl.when(qi == 0)
    def _():
        m0 = lax.broadcasted_iota(jnp.int32, (k_ref.shape[1], LANES), 1) < HEAD_DIM
        for p in range(n_streams):
            e, j = divmod(p, n_pairs)
            kj = k_ref[e, :, j * LANES:(j + 1) * LANES]
            vj = v_ref[e, :, j * LANES:(j + 1) * LANES]
            zero = jnp.zeros_like(kj)
            km_ref[2 * p] = jnp.where(m0, kj, zero)
            km_ref[2 * p + 1] = jnp.where(m0, zero, kj)
            vm_ref[2 * p] = jnp.where(m0, vj, zero)
            vm_ref[2 * p + 1] = jnp.where(m0, zero, vj)

    rowi = lax.broadcasted_iota(jnp.int32, (t, t), 0)
    coli = lax.broadcasted_iota(jnp.int32, (t, t), 1)
    causal = coli < rowi
    causal2 = jnp.concatenate([causal, causal], axis=1)
    later2 = later_ref[...]
    qb = [q_ref[p // n_pairs, :, (p % n_pairs) * LANES:(p % n_pairs + 1) * LANES] * (HEAD_DIM ** -0.5)
          for p in range(n_streams)]

    def both_heads(ref, kb, p):
        off = kb * t if isinstance(kb, int) else pl.multiple_of(kb * t, t)
        return jnp.concatenate([ref[2 * p, pl.ds(off, t), :], ref[2 * p + 1, pl.ds(off, t), :]], axis=0)

    def scores(kbs):
        return tuple(lax.dot_general(qb[p], both_heads(km_ref, kb, p), (((1,), (1,)), ((), ())),
                                     preferred_element_type=F32) for kb in kbs for p in range(n_streams))

    def group(kbs, zs, accs, carries, diagonal):
        units = [(kb, p) for kb in kbs for p in range(n_streams)]
        sps = [_softplus(z) for z in zs]
        if diagonal:
            sps = [jnp.where(causal2, sp, 0.0) for sp in sps]
        locs = [jnp.dot(sp.astype(BF16), later2, preferred_element_type=F32) for sp in sps]
        accs, carries = list(accs), list(carries)
        for i, (kb, p) in enumerate(units):
            halves = []
            for m in range(2):
                sl = slice(m * t, (m + 1) * t)
                att = jnp.exp(zs[i][:, sl] - locs[i][:, sl] - carries[2 * p + m])
                if diagonal:
                    att = jnp.where(causal, att, 0.0)
                halves.append(att.astype(BF16))
                carries[2 * p + m] = carries[2 * p + m] + jnp.sum(sps[i][:, sl], axis=1, keepdims=True)
            accs[p] = accs[p] + jnp.dot(jnp.concatenate(halves, axis=1), both_heads(vm_ref, kb, p),
                                        preferred_element_type=F32)
        return tuple(accs), tuple(carries)

    state = (tuple(jnp.zeros((t, LANES), F32) for _ in range(n_streams)),
             tuple(jnp.zeros((t, 1), F32) for _ in range(2 * n_streams)))
    state = group([qi], scores([qi]), *state, True)

    def run(first, n_iters, width, state):
        def body(it, c):
            kbs = [first - width * it - w for w in range(width)]
            zs_next = scores(kbs[0:1])
            for w in range(width):
                zs = zs_next
                if w + 1 < width:
                    zs_next = scores(kbs[w + 1:w + 2])
                c = group(kbs[w:w + 1], zs, *c, False)
            return c
        return lax.fori_loop(0, n_iters, body, state)

    n4 = jnp.right_shift(qi, 2)
    n2 = jnp.bitwise_and(jnp.right_shift(qi, 1), 1)
    state = run(qi - 1, n4, 4, state)
    state = run(qi - 1 - 4 * n4, n2, 2, state)
    state = run(0, jnp.bitwise_and(qi, 1), 1, state)
    accs = state[0]

    for p in range(n_streams):
        e, j = divmod(p, n_pairs)
        o = accs[p]
        hi, lo = _split2(o * o)
        ss = jnp.dot(hi, bd_ref[...], preferred_element_type=F32) + jnp.dot(lo, bd_ref[...], preferred_element_type=F32)
        o_ref[e, :, j * LANES:(j + 1) * LANES] = (o * lax.rsqrt(ss * (1.0 / HEAD_DIM) + RMS_EPS)
                                                  * g_ref[:, j * LANES:(j + 1) * LANES]).astype(BF16)


def _sb_group(ps, norm_g, bd128, layer, *, nb=2):
    b, lp, _ = ps.shape
    assert lp % SB_TILE == 0 and b % nb == 0
    n_pairs = D_SB // LANES
    idx = jnp.arange(2 * SB_TILE)
    later2 = ((idx[:, None] >= idx[None, :]) & (idx[:, None] // SB_TILE == idx[None, :] // SB_TILE)).astype(BF16)
    masked_copies = pltpu.VMEM((2 * nb * n_pairs, lp, LANES), BF16)
    return pl.pallas_call(
        functools.partial(_sb_kernel, nb=nb),
        grid=(b // nb, lp // SB_TILE),
        in_specs=[
            pl.BlockSpec((nb, SB_TILE, D_SB), lambda i, q: (i, q, 0)),
            pl.BlockSpec((nb, lp, D_SB), lambda i, q: (i, 0, 1)),
            pl.BlockSpec((nb, lp, D_SB), lambda i, q: (i, 0, 2)),
            pl.BlockSpec((None, 1, D_SB), lambda i, q: (layer, 0, 0)),
            pl.BlockSpec((LANES, LANES), lambda i, q: (0, 0)),
            pl.BlockSpec(later2.shape, lambda i, q: (0, 0)),
        ],
        out_specs=pl.BlockSpec((nb, SB_TILE, D_SB), lambda i, q: (i, q, 0)),
        out_shape=jax.ShapeDtypeStruct((b, lp, D_SB), BF16),
        scratch_shapes=[masked_copies, masked_copies],
        compiler_params=_cparams(2),
        name="sb_group",
    )(ps, ps, ps, norm_g, bd128, later2)


def _block_diag_ones(n):
    i = jnp.arange(n) // HEAD_DIM
    return (i[:, None] == i[None, :]).astype(BF16)


@jax.jit
def _trunk(x, meta, ffn1_norm, ffn1_w13, ffn1_w2, mix_norm, w_in, conv_w, conv_b, conv_ln_g, conv_ln_b, rwkv_mu,
           rwkv_w0, rwkv_wB, rwkv_a0, rwkv_aB, rwkv_gB, rwkv_kk, rwkv_ka, rwkv_rk, rwkv_ln_g, rwkv_ln_b, sb_norm,
           w_out, ffn2_norm, ffn2_w13, ffn2_w2, final_norm):
    bsz, seq, d = x.shape
    depth = w_in.shape[0]
    l_real = N_META + seq
    lp = -(-l_real // SEQ_ALIGN) * SEQ_ALIGN
    meta_b = jnp.broadcast_to(meta.astype(x.dtype)[None], (bsz, N_META, d))
    h = jnp.concatenate([meta_b, x, jnp.zeros((bsz, lp - l_real, d), x.dtype)], axis=1).reshape(bsz * lp, d)

    row3 = lambda p: p.reshape(depth, 1, -1)
    ffn1_w13, ffn1_w2, ffn2_w13, ffn2_w2 = (w.astype(BF16) for w in (ffn1_w13, ffn1_w2, ffn2_w13, ffn2_w2))
    w_in, w_out = w_in.astype(BF16), w_out.astype(BF16)
    ffn1_norm, mix_norm, ffn2_norm, sb_norm = row3(ffn1_norm), row3(mix_norm), row3(ffn2_norm), row3(sb_norm)
    conv_b, conv_ln_g, conv_ln_b = row3(conv_b), row3(conv_ln_g), row3(conv_ln_b)
    rwkv_vecs = [row3(p) for p in (rwkv_mu, rwkv_w0, rwkv_a0, rwkv_kk, rwkv_ka, rwkv_rk, rwkv_ln_g, rwkv_ln_b)]
    mu, w0, a0, k_k, k_a, r_k, ln_g, ln_b = rwkv_vecs
    wb_ext = jnp.pad(rwkv_wB, ((0, 0), (0, AAA_LORA), (0, 0))).astype(BF16)
    ab_ext = jnp.pad(rwkv_aB, ((0, 0), (DECAY_LORA, 0), (0, 0))).astype(BF16)
    gb = rwkv_gB.astype(BF16)
    bd_rwkv = _block_diag_ones(D_RWKV)
    bd_pair = _block_diag_ones(LANES)

    for l in range(depth):
        h = _ffn(h, ffn1_norm, ffn1_w13, ffn1_w2, l)
        pc, pr, ps = _inproj(h, mix_norm, w_in, l)
        yc = _conv_group(pc.reshape(bsz, lp, -1), conv_w, conv_b, conv_ln_g, conv_ln_b, l)
        yr = _rwkv_group(pr.reshape(bsz, lp, -1), mu, w0, wb_ext, a0, ab_ext, gb, k_k, k_a, r_k, ln_g, ln_b,
                         bd_rwkv, l)
        ys = _sb_group(ps.reshape(bsz, lp, -1), sb_norm, bd_pair, l)
        mix = (yc.reshape(bsz * lp, -1), yr.reshape(bsz * lp, -1), ys.reshape(bsz * lp, -1), w_out)
        h = _ffn(h, ffn2_norm, ffn2_w13, ffn2_w2, l, mix=mix,
                 final_g=final_norm.reshape(1, d) if l == depth - 1 else None)
    out = h
    return out.reshape(bsz, lp, d)[:, N_META:l_real]


def kernel(x, meta, ffn1_norm, ffn1_w13, ffn1_w2, mix_norm, w_in, conv_w, conv_b, conv_ln_g, conv_ln_b, rwkv_mu,
           rwkv_w0, rwkv_wB, rwkv_a0, rwkv_aB, rwkv_gB, rwkv_kk, rwkv_ka, rwkv_rk, rwkv_ln_g, rwkv_ln_b, sb_norm,
           w_out, ffn2_norm, ffn2_w13, ffn2_w2, final_norm):
    return _trunk(x, meta, ffn1_norm, ffn1_w13, ffn1_w2, mix_norm, w_in, conv_w, conv_b, conv_ln_g, conv_ln_b,
                  rwkv_mu, rwkv_w0, rwkv_wB, rwkv_a0, rwkv_aB, rwkv_gB, rwkv_kk, rwkv_ka, rwkv_rk, rwkv_ln_g,
                  rwkv_ln_b, sb_norm, w_out, ffn2_norm, ffn2_w13, ffn2_w2, final_norm)
```

```python
import functools

import jax
import jax.numpy as jnp
from jax import lax
from jax.experimental import pallas as pl
from jax.experimental.pallas import tpu as pltpu

F32 = jnp.float32
BF16 = jnp.bfloat16

N_META = 16
HEAD_DIM = 64
D_CONV = 256
CONV_WIDTH = 31
D_RWKV = 384
D_SB = 384
DECAY_LORA = 64
AAA_LORA = 64
GATE_LORA = 128
D_RWKV_IN = 3 * D_RWKV + DECAY_LORA + AAA_LORA + GATE_LORA
RMS_EPS = 1e-6
LN_EPS = 1e-5
GN_EPS = 64e-5

LANES = 128
SEQ_ALIGN = 128
RWKV_CHUNK = 64
VMEM_LIMIT = 56 * 1024 * 1024


def _cparams(n_axes):
    return pltpu.CompilerParams(dimension_semantics=("arbitrary",) * n_axes, vmem_limit_bytes=VMEM_LIMIT)


def _dot(a, b):
    return jnp.dot(a.astype(BF16), b.astype(BF16), preferred_element_type=F32)


def _dot_nt(a, b):
    return lax.dot_general(a.astype(BF16), b.astype(BF16), (((1,), (1,)), ((), ())), preferred_element_type=F32)


def _split3(x):
    h1 = x.astype(BF16)
    r1 = x - h1.astype(F32)
    h2 = r1.astype(BF16)
    return h1, h2, (r1 - h2.astype(F32)).astype(BF16)


def _split2(x):
    hi = x.astype(BF16)
    return hi, (x - hi.astype(F32)).astype(BF16)


def _sigmoid(x):
    return 1.0 / (1.0 + jnp.exp(-x))


def _softplus(x):
    return jnp.maximum(x, 0.0) + jnp.log(1.0 + jnp.exp(-jnp.abs(x)))


def _rms_norm_rows(x, g):
    ms = jnp.mean(x * x, axis=-1, keepdims=True)
    return x * lax.rsqrt(ms + RMS_EPS) * g


def _ffn_kernel(h_ref, yc_ref, yr_ref, ys_ref, wo_ref, g_ref, w13_ref, w2_ref, fg_ref, o_ref, act_ref, *,
                d_ff, tf, mix, final):
    hin = h_ref[...]
    if mix:
        r0 = D_CONV
        r1 = D_CONV + D_RWKV
        hin = hin + jnp.dot(yc_ref[...], wo_ref[0:r0, :], preferred_element_type=F32)
        hin = hin + jnp.dot(yr_ref[...], wo_ref[r0:r1, :], preferred_element_type=F32)
        hin = hin + jnp.dot(ys_ref[...], wo_ref[r1:, :], preferred_element_type=F32)
    o_ref[...] = hin
    xn = _rms_norm_rows(hin, g_ref[...]).astype(BF16)
    for c0 in range(0, d_ff, tf):
        gate = jnp.dot(xn, w13_ref[:, c0:c0 + tf], preferred_element_type=F32)
        up = jnp.dot(xn, w13_ref[:, d_ff + c0:d_ff + c0 + tf], preferred_element_type=F32)
        act_ref[:, c0:c0 + tf] = (gate * _sigmoid(gate) * up).astype(BF16)
    out = o_ref[...] + 0.5 * jnp.dot(act_ref[...], w2_ref[...], preferred_element_type=F32)
    if final:
        out = _rms_norm_rows(out, fg_ref[...])
    o_ref[...] = out


def _ffn(h2d, norm_g, w13, w2, layer, *, mix=None, final_g=None, tm=512, tf=256):
    m, d = h2d.shape
    d_ff = w2.shape[1]
    assert m % tm == 0 and d_ff % tf == 0
    resident = dict(pipeline_mode=pl.Buffered(1))
    rows = lambda w: pl.BlockSpec((tm, w), lambda i: (i, 0))
    if mix is None:
        mix_args = [jnp.zeros((tm, w), BF16) for w in (D_CONV, D_RWKV, D_SB)] + [jnp.zeros((1, 16, d), BF16)]
        mix_specs = [pl.BlockSpec((tm, w), lambda i: (0, 0)) for w in (D_CONV, D_RWKV, D_SB)]
        wo_layer = 0
    else:
        mix_args = list(mix)
        mix_specs = [rows(D_CONV), rows(D_RWKV), rows(D_SB)]
        wo_layer = layer
    mix_specs.append(pl.BlockSpec((None,) + mix_args[3].shape[1:], lambda i: (wo_layer, 0, 0), **resident))
    fg = jnp.zeros((1, d), F32) if final_g is None else final_g
    return pl.pallas_call(
        functools.partial(_ffn_kernel, d_ff=d_ff, tf=tf, mix=mix is not None, final=final_g is not None),
        grid=(m // tm,),
        in_specs=[rows(d)] + mix_specs + [
            pl.BlockSpec((None, 1, d), lambda i: (layer, 0, 0)),
            pl.BlockSpec((None, d, 2 * d_ff), lambda i: (layer, 0, 0), **resident),
            pl.BlockSpec((None, d_ff, d), lambda i: (layer, 0, 0), **resident),
            pl.BlockSpec((1, d), lambda i: (0, 0)),
        ],
        out_specs=rows(d),
        out_shape=jax.ShapeDtypeStruct((m, d), F32),
        scratch_shapes=[pltpu.VMEM((tm, d_ff), BF16)],
        compiler_params=_cparams(1),
        name="ffn",
    )(h2d, *mix_args, norm_g, w13, w2, fg)


def _inproj_kernel(h_ref, g_ref, w_ref, pc_ref, pr_ref, ps_ref):
    xn = _rms_norm_rows(h_ref[...], g_ref[...]).astype(BF16)
    c0 = 2 * D_CONV
    c1 = c0 + D_RWKV_IN
    pc_ref[...] = jnp.dot(xn, w_ref[:, 0:c0], preferred_element_type=F32)
    pr_ref[...] = jnp.dot(xn, w_ref[:, c0:c1], preferred_element_type=F32)
    ps_ref[...] = jnp.dot(xn, w_ref[:, c1:], preferred_element_type=F32).astype(BF16)


def _inproj(h2d, norm_g, w_in, layer, *, tm=512):
    m, d = h2d.shape
    assert m % tm == 0
    d_in = w_in.shape[2]
    widths = (2 * D_CONV, D_RWKV_IN, 3 * D_SB)
    return pl.pallas_call(
        _inproj_kernel,
        grid=(m // tm,),
        in_specs=[
            pl.BlockSpec((tm, d), lambda i: (i, 0)),
            pl.BlockSpec((None, 1, d), lambda i: (layer, 0, 0)),
            pl.BlockSpec((None, d, d_in), lambda i: (layer, 0, 0)),
        ],
        out_specs=[pl.BlockSpec((tm, w), lambda i: (i, 0)) for w in widths],
        out_shape=[jax.ShapeDtypeStruct((m, w), dt) for w, dt in zip(widths, (F32, F32, BF16))],
        compiler_params=_cparams(1),
        name="inproj",
    )(h2d, norm_g, w_in)


CONV_PAD = 32
CONV_SUB = 64


def _conv_tile(e, p_ref, w_ref, b_ref, lg_ref, lb_ref, o_ref, buf_ref, shift_ref, tt):
    val = p_ref[e, :, 0:D_CONV]
    gate = p_ref[e, :, D_CONV:2 * D_CONV]
    buf_ref[e, CONV_PAD:CONV_PAD + tt, :] = val * _sigmoid(gate)
    first = CONV_PAD - (CONV_WIDTH - 1)
    for b in range(8):
        span = tt + 8 * ((CONV_WIDTH - 1 - b) // 8)
        shift_ref[e, b, 0:span, :] = buf_ref[e, first + b:first + b + span, :]
    for s in range(tt // CONV_SUB):
        acc = jnp.zeros((CONV_SUB, D_CONV), F32)
        for j in range(CONV_WIDTH):
            r0 = s * CONV_SUB + 8 * (j // 8)
            acc = acc + w_ref[j:j + 1, :] * shift_ref[e, j % 8, r0:r0 + CONV_SUB, :]
        y = acc + b_ref[...]
        mu = jnp.mean(y, axis=-1, keepdims=True)
        dlt = y - mu
        var = jnp.mean(dlt * dlt, axis=-1, keepdims=True)
        yn = dlt * lax.rsqrt(var + LN_EPS) * lg_ref[...] + lb_ref[...]
        o_ref[e, s * CONV_SUB:(s + 1) * CONV_SUB, :] = (yn * _sigmoid(yn)).astype(BF16)
    buf_ref[e, 0:CONV_PAD, :] = buf_ref[e, tt:tt + CONV_PAD, :]


def _rwkv_conv_kernel(p_ref, mu_ref, w0_ref, wb_ref, a0_ref, ab_ref, gb_ref, kk_ref, ka_ref, rk_ref, lg_ref, lb_ref,
                      bd_ref, tri_ref, pc_ref, cw_ref, cb_ref, clg_ref, clb_ref, o_ref, oc_ref,
                      st_ref, carry_ref, buf_ref, shift_ref, *, nb, tt):
    c = RWKV_CHUNK
    c2 = 2 * c
    n_pairs = D_RWKV // LANES
    chunks_per_seq = tt // c
    n_chunks = nb * chunks_per_seq
    t = pl.program_id(1)

    @pl.when(t == 0)
    def _():
        st_ref[...] = jnp.zeros_like(st_ref)
        carry_ref[...] = jnp.zeros_like(carry_ref)
        buf_ref[:, 0:CONV_PAD, :] = jnp.zeros((nb, CONV_PAD, D_CONV), F32)


    x = p_ref[...].reshape(nb * tt, D_RWKV_IN)
    row = lax.broadcasted_iota(jnp.int32, x.shape, 0)
    prev = pltpu.roll(x, 1, 0)
    for e in range(nb):
        prev = jnp.where(row == e * tt, carry_ref[e], prev)
        carry_ref[e] = x[(e + 1) * tt - 1:(e + 1) * tt, :]
    xs = x + mu_ref[...] * (prev - x)

    r = xs[:, 0:D_RWKV]
    k = xs[:, D_RWKV:2 * D_RWKV]
    v = xs[:, 2 * D_RWKV:3 * D_RWKV]
    lora_in = xs[:, 3 * D_RWKV:3 * D_RWKV + DECAY_LORA + AAA_LORA]
    gd = xs[:, 3 * D_RWKV + DECAY_LORA + AAA_LORA:]

    bd = bd_ref[...]

    def head_sum(z):
        hi, lo = _split2(z)
        return jnp.dot(hi, bd, preferred_element_type=F32) + jnp.dot(lo, bd, preferred_element_type=F32)

    w_log = -_softplus(-(w0_ref[...] + _dot(jnp.tanh(lora_in), wb_ref[...]))) - 0.5
    logw = -jnp.exp(w_log)
    a = _sigmoid(a0_ref[...] + _dot(lora_in, ab_ref[...]))
    g = _dot(_sigmoid(gd), gb_ref[...])
    kk = k * kk_ref[...]
    kk = kk / jnp.maximum(jnp.sqrt(head_sum(kk * kk)), 1e-12)
    kq = k * (1.0 + (a - 1.0) * ka_ref[...])
    na = -kk
    b = kk * a
    bonus = head_sum(r * kq * rk_ref[...]) * v

    tri = tri_ref[...]
    l1, l2, l3 = _split3(logw)
    cl = (jnp.dot(tri, l1, preferred_element_type=F32) + jnp.dot(tri, l2, preferred_element_type=F32)
          + jnp.dot(tri, l3, preferred_element_type=F32))
    cl_ends = [cl[(s + 1) * c - 1:(s + 1) * c, :] for s in range(n_chunks)]
    cl_end = jnp.concatenate([jnp.broadcast_to(e, (c, D_RWKV)) for e in cl_ends], axis=0)
    e_pos = jnp.exp(cl)
    e_neg = jnp.exp(-cl)
    e_end = jnp.exp(cl_end - cl)
    at_all = jnp.exp(cl - logw) * na
    bt_all = b * e_neg
    kt_all = kq * e_neg
    rt_all = r * e_pos
    bp_all = b * e_end
    kp_all = kq * e_end

    lane = lax.broadcasted_iota(jnp.int32, (c, LANES), 1)
    lane2 = lax.broadcasted_iota(jnp.int32, (c2, LANES), 1)
    m0 = lane < HEAD_DIM
    m0_2 = lane2 < HEAD_DIM
    ri = lax.broadcasted_iota(jnp.int32, (c2, c2), 0)
    cj = lax.broadcasted_iota(jnp.int32, (c2, c2), 1)
    ti = jnp.bitwise_and(ri, c - 1)
    sj = jnp.bitwise_and(cj, c - 1)
    mask_z = (ti > sj) | ((ri >= c) & (ti == sj))
    left_top = lax.broadcasted_iota(jnp.int32, (c, c2), 1) < c
    blockdiag = jnp.bitwise_and(ri, c) == jnp.bitwise_and(cj, c)
    eye = (ri == cj)
    zero = jnp.zeros((c, LANES), F32)
    rows = lambda parts: jnp.concatenate(parts, axis=0)
    cols = lambda parts: jnp.concatenate(parts, axis=1)

    units = [(s, j) for s in range(n_chunks) for j in range(n_pairs)]

    def tile(arr, u):
        s, j = u
        return arr[s * c:(s + 1) * c, j * LANES:(j + 1) * LANES]

    at = [tile(at_all, u) for u in units]
    rt = [tile(rt_all, u) for u in units]
    vv = [tile(v, u) for u in units]

    z0, z1 = [], []
    for i, u in enumerate(units):
        xx = rows([at[i], rt[i]])
        bt, kt = tile(bt_all, u), tile(kt_all, u)
        z0.append(jnp.where(mask_z, _dot_nt(jnp.where(m0_2, xx, 0.0), rows([bt, kt])), 0.0))
        z1.append(jnp.where(mask_z, _dot_nt(jnp.where(m0_2, 0.0, xx), rows([kt, bt])), 0.0))

    pw = [rows([jnp.where(left_top, z0[i][0:c], 0.0), jnp.where(left_top, 0.0, z1[i][0:c])]) for i in range(len(units))]
    tinv = [jnp.where(eye, 1.0, 0.0) + p for p in pw]
    for _ in range(5):
        pw = [_dot(p, p) for p in pw]
        tinv = [ti_ + _dot(ti_, p) for ti_, p in zip(tinv, pw)]

    akv = []
    for i in range(len(units)):
        v2 = rows([vv[i], vv[i]])
        akv.append(jnp.where(m0, _dot(jnp.where(left_top, 0.0, z0[i][0:c]), v2),
                             _dot(jnp.where(left_top, z1[i][0:c], 0.0), v2)))

    ah, uv = [], []
    for i in range(len(units)):
        wa = rows([jnp.where(m0, at[i], 0.0), jnp.where(m0, 0.0, at[i])])
        wv = rows([jnp.where(m0, akv[i], 0.0), jnp.where(m0, 0.0, akv[i])])
        tw = _dot(tinv[i], cols([wa, wv]))
        ah.append(tw[0:c, 0:LANES] + tw[c:c2, 0:LANES])
        uv.append(tw[0:c, LANES:] + tw[c:c2, LANES:])

    rh, ov, gmat, hmat = [], [], [], []
    for i, u in enumerate(units):
        rhs0 = cols([rows([ah[i], zero]), rows([uv[i], vv[i]])])
        rhs1 = cols([rows([zero, ah[i]]), rows([vv[i], uv[i]])])
        res0 = _dot(z0[i][c:c2], rhs0)
        res1 = _dot(z1[i][c:c2], rhs1)
        rh.append(rt[i] + jnp.where(m0, res0[:, 0:LANES], res1[:, 0:LANES]))
        ov.append(jnp.where(m0, res0[:, LANES:], res1[:, LANES:]))
        ypt = rows([tile(bp_all, u), tile(kp_all, u)]).T
        gh = _dot(ypt, rhs0)
        s, j = u
        p_end = jnp.exp(cl_ends[s][:, j * LANES:(j + 1) * LANES])
        gmat.append(jnp.where(blockdiag, gh[:, 0:LANES], 0.0)
                    + jnp.where(eye, jnp.broadcast_to(p_end, (c2, LANES)), 0.0))
        hmat.append(jnp.where(blockdiag, gh[:, LANES:], 0.0))

    states = [st_ref[n] for n in range(nb * n_pairs)]
    out_rows = []
    for s in range(n_chunks):
        out_pairs = []
        for j in range(n_pairs):
            i = s * n_pairs + j
            n = (s // chunks_per_seq) * n_pairs + j
            out_pairs.append(_dot(rh[i], states[n]) + ov[i])
            states[n] = _dot(gmat[i], states[n]) + hmat[i]
        out_rows.append(cols(out_pairs))
    for n in range(nb * n_pairs):
        st_ref[n] = states[n]

    o = rows(out_rows)
    mean = head_sum(o) * (1.0 / HEAD_DIM)
    dlt = o - mean
    var = head_sum(dlt * dlt) * (1.0 / HEAD_DIM)
    on = dlt * lax.rsqrt(var + GN_EPS) * lg_ref[...] + lb_ref[...]
    o_ref[...] = ((on + bonus) * g).astype(BF16).reshape(nb, tt, D_RWKV)

    for e in range(nb):
        _conv_tile(e, pc_ref, cw_ref, cb_ref, clg_ref, clb_ref, oc_ref, buf_ref, shift_ref, tt)


def _rwkv_conv_group(pr, pc, rwkv_params, conv_params, bd, layer, *, nb=2, tt=128):
    b, lp, _ = pr.shape
    assert b % nb == 0 and lp % tt == 0 and tt % RWKV_CHUNK == 0 and tt % CONV_SUB == 0
    n_pairs = D_RWKV // LANES
    idx = jnp.arange(nb * tt)
    tri = ((idx[:, None] >= idx[None, :]) & (idx[:, None] // RWKV_CHUNK == idx[None, :] // RWKV_CHUNK)).astype(BF16)

    def par(arr):
        return pl.BlockSpec((None,) + arr.shape[1:], lambda i, t: (layer,) + (0,) * (arr.ndim - 1))

    def const(arr):
        return pl.BlockSpec(arr.shape, lambda i, t: (0,) * arr.ndim)

    tile = lambda w: pl.BlockSpec((nb, tt, w), lambda i, t: (i, t, 0))
    return pl.pallas_call(
        functools.partial(_rwkv_conv_kernel, nb=nb, tt=tt),
        grid=(b // nb, lp // tt),
        in_specs=[tile(D_RWKV_IN)] + [par(p) for p in rwkv_params] + [const(bd), const(tri)]
        + [tile(2 * D_CONV)] + [par(p) for p in conv_params],
        out_specs=[tile(D_RWKV), tile(D_CONV)],
        out_shape=[jax.ShapeDtypeStruct((b, lp, D_RWKV), BF16), jax.ShapeDtypeStruct((b, lp, D_CONV), BF16)],
        scratch_shapes=[pltpu.VMEM((nb * n_pairs, LANES, LANES), F32), pltpu.VMEM((nb, 1, D_RWKV_IN), F32),
                        pltpu.VMEM((nb, tt + CONV_PAD, D_CONV), F32),
                        pltpu.VMEM((nb, 8, tt + 8 * ((CONV_WIDTH - 1) // 8), D_CONV), F32)],
        compiler_params=_cparams(2),
        name="rwkv_conv_group",
    )(pr, *rwkv_params, bd, tri, pc, *conv_params)


SB_TILE = 128


def _sb_kernel(q_ref, k_ref, v_ref, g_ref, bd_ref, later_ref, o_ref, km_ref, vm_ref, *, nb):
    t = SB_TILE
    n_pairs = D_SB // LANES
    n_streams = nb * n_pairs
    qi = pl.program_id(1)

    @pl.when(qi == 0)
    def _():
        m0 = lax.broadcasted_iota(jnp.int32, (k_ref.shape[1], LANES), 1) < HEAD_DIM
        for p in range(n_streams):
            e, j = divmod(p, n_pairs)
            kj = k_ref[e, :, j * LANES:(j + 1) * LANES]
            vj = v_ref[e, :, j * LANES:(j + 1) * LANES]
            zero = jnp.zeros_like(kj)
            km_ref[2 * p] = jnp.where(m0, kj, zero)
            km_ref[2 * p + 1] = jnp.where(m0, zero, kj)
            vm_ref[2 * p] = jnp.where(m0, vj, zero)
            vm_ref[2 * p + 1] = jnp.where(m0, zero, vj)

    rowi = lax.broadcasted_iota(jnp.int32, (t, t), 0)
    coli = lax.broadcasted_iota(jnp.int32, (t, t), 1)
    causal = coli < rowi
    causal2 = jnp.concatenate([causal, causal], axis=1)
    later2 = later_ref[...]
    qb = [q_ref[p // n_pairs, :, (p % n_pairs) * LANES:(p % n_pairs + 1) * LANES] * (HEAD_DIM ** -0.5)
          for p in range(n_streams)]

    def both_heads(ref, kb, p):
        off = kb * t if isinstance(kb, int) else pl.multiple_of(kb * t, t)
        return jnp.concatenate([ref[2 * p, pl.ds(off, t), :], ref[2 * p + 1, pl.ds(off, t), :]], axis=0)

    def scores(kbs):
        return tuple(lax.dot_general(qb[p], both_heads(km_ref, kb, p), (((1,), (1,)), ((), ())),
                                     preferred_element_type=F32) for kb in kbs for p in range(n_streams))

    def group(kbs, zs, accs, carries, diagonal):
        units = [(kb, p) for kb in kbs for p in range(n_streams)]
        sps = [_softplus(z) for z in zs]
        if diagonal:
            sps = [jnp.where(causal2, sp, 0.0) for sp in sps]
        locs = [jnp.dot(sp.astype(BF16), later2, preferred_element_type=F32) for sp in sps]
        accs, carries = list(accs), list(carries)
        for i, (kb, p) in enumerate(units):
            halves = []
            for m in range(2):
                sl = slice(m * t, (m + 1) * t)
                att = jnp.exp(zs[i][:, sl] - locs[i][:, sl] - carries[2 * p + m])
                if diagonal:
                    att = jnp.where(causal, att, 0.0)
                halves.append(att.astype(BF16))
                carries[2 * p + m] = carries[2 * p + m] + jnp.sum(sps[i][:, sl], axis=1, keepdims=True)
            accs[p] = accs[p] + jnp.dot(jnp.concatenate(halves, axis=1), both_heads(vm_ref, kb, p),
                                        preferred_element_type=F32)
        return tuple(accs), tuple(carries)

    state = (tuple(jnp.zeros((t, LANES), F32) for _ in range(n_streams)),
             tuple(jnp.zeros((t, 1), F32) for _ in range(2 * n_streams)))
    state = group([qi], scores([qi]), *state, True)

    def run(first, n_iters, width, state):
        def body(it, c):
            kbs = [first - width * it - w for w in range(width)]
            zs_next = scores(kbs[0:1])
            for w in range(width):
                zs = zs_next
                if w + 1 < width:
                    zs_next = scores(kbs[w + 1:w + 2])
                c = group(kbs[w:w + 1], zs, *c, False)
            return c
        return lax.fori_loop(0, n_iters, body, state)

    n4 = jnp.right_shift(qi, 2)
    n2 = jnp.bitwise_and(jnp.right_shift(qi, 1), 1)
    state = run(qi - 1, n4, 4, state)
    state = run(qi - 1 - 4 * n4, n2, 2, state)
    state = run(0, jnp.bitwise_and(qi, 1), 1, state)
    accs = state[0]

    for p in range(n_streams):
        e, j = divmod(p, n_pairs)
        o = accs[p]
        hi, lo = _split2(o * o)
        ss = jnp.dot(hi, bd_ref[...], preferred_element_type=F32) + jnp.dot(lo, bd_ref[...], preferred_element_type=F32)
        o_ref[e, :, j * LANES:(j + 1) * LANES] = (o * lax.rsqrt(ss * (1.0 / HEAD_DIM) + RMS_EPS)
                                                  * g_ref[:, j * LANES:(j + 1) * LANES]).astype(BF16)


def _sb_group(ps, norm_g, bd128, layer, *, nb=2):
    b, lp, _ = ps.shape
    assert lp % SB_TILE == 0 and b % nb == 0
    n_pairs = D_SB // LANES
    idx = jnp.arange(2 * SB_TILE)
    later2 = ((idx[:, None] >= idx[None, :]) & (idx[:, None] // SB_TILE == idx[None, :] // SB_TILE)).astype(BF16)
    masked_copies = pltpu.VMEM((2 * nb * n_pairs, lp, LANES), BF16)
    return pl.pallas_call(
        functools.partial(_sb_kernel, nb=nb),
        grid=(b // nb, lp // SB_TILE),
        in_specs=[
            pl.BlockSpec((nb, SB_TILE, D_SB), lambda i, q: (i, q, 0)),
            pl.BlockSpec((nb, lp, D_SB), lambda i, q: (i, 0, 1)),
            pl.BlockSpec((nb, lp, D_SB), lambda i, q: (i, 0, 2)),
            pl.BlockSpec((None, 1, D_SB), lambda i, q: (layer, 0, 0)),
            pl.BlockSpec((LANES, LANES), lambda i, q: (0, 0)),
            pl.BlockSpec(later2.shape, lambda i, q: (0, 0)),
        ],
        out_specs=pl.BlockSpec((nb, SB_TILE, D_SB), lambda i, q: (i, q, 0)),
        out_shape=jax.ShapeDtypeStruct((b, lp, D_SB), BF16),
        scratch_shapes=[masked_copies, masked_copies],
        compiler_params=_cparams(2),
        name="sb_group",
    )(ps, ps, ps, norm_g, bd128, later2)


def _block_diag_ones(n):
    i = jnp.arange(n) // HEAD_DIM
    return (i[:, None] == i[None, :]).astype(BF16)


@jax.jit
def _trunk(x, meta, ffn1_norm, ffn1_w13, ffn1_w2, mix_norm, w_in, conv_w, conv_b, conv_ln_g, conv_ln_b, rwkv_mu,
           rwkv_w0, rwkv_wB, rwkv_a0, rwkv_aB, rwkv_gB, rwkv_kk, rwkv_ka, rwkv_rk, rwkv_ln_g, rwkv_ln_b, sb_norm,
           w_out, ffn2_norm, ffn2_w13, ffn2_w2, final_norm):
    bsz, seq, d = x.shape
    depth = w_in.shape[0]
    l_real = N_META + seq
    lp = -(-l_real // SEQ_ALIGN) * SEQ_ALIGN
    meta_b = jnp.broadcast_to(meta.astype(x.dtype)[None], (bsz, N_META, d))
    h = jnp.concatenate([meta_b, x, jnp.zeros((bsz, lp - l_real, d), x.dtype)], axis=1).reshape(bsz * lp, d)

    row3 = lambda p: p.reshape(depth, 1, -1)
    ffn1_w13, ffn1_w2, ffn2_w13, ffn2_w2 = (w.astype(BF16) for w in (ffn1_w13, ffn1_w2, ffn2_w13, ffn2_w2))
    w_in, w_out = w_in.astype(BF16), w_out.astype(BF16)
    ffn1_norm, mix_norm, ffn2_norm, sb_norm = row3(ffn1_norm), row3(mix_norm), row3(ffn2_norm), row3(sb_norm)
    conv_b, conv_ln_g, conv_ln_b = row3(conv_b), row3(conv_ln_g), row3(conv_ln_b)
    rwkv_vecs = [row3(p) for p in (rwkv_mu, rwkv_w0, rwkv_a0, rwkv_kk, rwkv_ka, rwkv_rk, rwkv_ln_g, rwkv_ln_b)]
    mu, w0, a0, k_k, k_a, r_k, ln_g, ln_b = rwkv_vecs
    wb_ext = jnp.pad(rwkv_wB, ((0, 0), (0, AAA_LORA), (0, 0))).astype(BF16)
    ab_ext = jnp.pad(rwkv_aB, ((0, 0), (DECAY_LORA, 0), (0, 0))).astype(BF16)
    gb = rwkv_gB.astype(BF16)
    bd_rwkv = _block_diag_ones(D_RWKV)
    bd_pair = _block_diag_ones(LANES)

    for l in range(depth):
        h = _ffn(h, ffn1_norm, ffn1_w13, ffn1_w2, l)
        pc, pr, ps = _inproj(h, mix_norm, w_in, l)
        yr, yc = _rwkv_conv_group(pr.reshape(bsz, lp, -1), pc.reshape(bsz, lp, -1),
                                  (mu, w0, wb_ext, a0, ab_ext, gb, k_k, k_a, r_k, ln_g, ln_b),
                                  (conv_w, conv_b, conv_ln_g, conv_ln_b), bd_rwkv, l)
        ys = _sb_group(ps.reshape(bsz, lp, -1), sb_norm, bd_pair, l)
        mix = (yc.reshape(bsz * lp, -1), yr.reshape(bsz * lp, -1), ys.reshape(bsz * lp, -1), w_out)
        h = _ffn(h, ffn2_norm, ffn2_w13, ffn2_w2, l, mix=mix,
                 final_g=final_norm.reshape(1, d) if l == depth - 1 else None)
    out = h
    return out.reshape(bsz, lp, d)[:, N_META:l_real]


def kernel(x, meta, ffn1_norm, ffn1_w13, ffn1_w2, mix_norm, w_in, conv_w, conv_b, conv_ln_g, conv_ln_b, rwkv_mu,
           rwkv_w0, rwkv_wB, rwkv_a0, rwkv_aB, rwkv_gB, rwkv_kk, rwkv_ka, rwkv_rk, rwkv_ln_g, rwkv_ln_b, sb_norm,
           w_out, ffn2_norm, ffn2_w13, ffn2_w2, final_norm):
    return _trunk(x, meta, ffn1_norm, ffn1_w13, ffn1_w2, mix_norm, w_in, conv_w, conv_b, conv_ln_g, conv_ln_b,
                  rwkv_mu, rwkv_w0, rwkv_wB, rwkv_a0, rwkv_aB, rwkv_gB, rwkv_kk, rwkv_ka, rwkv_rk, rwkv_ln_g,
                  rwkv_ln_b, sb_norm, w_out, ffn2_norm, ffn2_w13, ffn2_w2, final_norm)
```

```python
import functools

import jax
import jax.numpy as jnp
from jax import lax
from jax.experimental import pallas as pl
from jax.experimental.pallas import tpu as pltpu

F32 = jnp.float32
BF16 = jnp.bfloat16

N_META = 16
HEAD_DIM = 64
D_CONV = 256
CONV_WIDTH = 31
D_RWKV = 384
D_SB = 384
DECAY_LORA = 64
AAA_LORA = 64
GATE_LORA = 128
D_RWKV_IN = 3 * D_RWKV + DECAY_LORA + AAA_LORA + GATE_LORA
RMS_EPS = 1e-6
LN_EPS = 1e-5
GN_EPS = 64e-5

LANES = 128
SEQ_ALIGN = 128
RWKV_CHUNK = 64
VMEM_LIMIT = 56 * 1024 * 1024


def _cparams(n_axes):
    return pltpu.CompilerParams(dimension_semantics=("arbitrary",) * n_axes, vmem_limit_bytes=VMEM_LIMIT)


def _dot(a, b):
    return jnp.dot(a.astype(BF16), b.astype(BF16), preferred_element_type=F32)


def _dot_nt(a, b):
    return lax.dot_general(a.astype(BF16), b.astype(BF16), (((1,), (1,)), ((), ())), preferred_element_type=F32)


def _split3(x):
    h1 = x.astype(BF16)
    r1 = x - h1.astype(F32)
    h2 = r1.astype(BF16)
    return h1, h2, (r1 - h2.astype(F32)).astype(BF16)


def _split2(x):
    hi = x.astype(BF16)
    return hi, (x - hi.astype(F32)).astype(BF16)


def _sigmoid(x):
    return 1.0 / (1.0 + jnp.exp(-x))


def _softplus(x):
    return jnp.maximum(x, 0.0) + jnp.log(1.0 + jnp.exp(-jnp.abs(x)))


def _rms_norm_rows(x, g):
    ms = jnp.mean(x * x, axis=-1, keepdims=True)
    return x * lax.rsqrt(ms + RMS_EPS) * g


def _ffn_kernel(h_ref, yc_ref, yr_ref, ys_ref, wo_ref, g_ref, w13_ref, w2_ref, fg_ref, o_ref, act_ref, *,
                d_ff, tf, mix, final):
    hin = h_ref[...]
    if mix:
        r0 = D_CONV
        r1 = D_CONV + D_RWKV
        hin = hin + jnp.dot(yc_ref[...], wo_ref[0:r0, :], preferred_element_type=F32)
        hin = hin + jnp.dot(yr_ref[...], wo_ref[r0:r1, :], preferred_element_type=F32)
        hin = hin + jnp.dot(ys_ref[...], wo_ref[r1:, :], preferred_element_type=F32)
    o_ref[...] = hin
    xn = _rms_norm_rows(hin, g_ref[...]).astype(BF16)
    for c0 in range(0, d_ff, tf):
        gate = jnp.dot(xn, w13_ref[:, c0:c0 + tf], preferred_element_type=F32)
        up = jnp.dot(xn, w13_ref[:, d_ff + c0:d_ff + c0 + tf], preferred_element_type=F32)
        act_ref[:, c0:c0 + tf] = (gate * _sigmoid(gate) * up).astype(BF16)
    out = o_ref[...] + 0.5 * jnp.dot(act_ref[...], w2_ref[...], preferred_element_type=F32)
    if final:
        out = _rms_norm_rows(out, fg_ref[...])
    o_ref[...] = out


def _ffn(h2d, norm_g, w13, w2, layer, *, mix=None, final_g=None, tm=512, tf=256):
    m, d = h2d.shape
    d_ff = w2.shape[1]
    assert m % tm == 0 and d_ff % tf == 0
    resident = dict(pipeline_mode=pl.Buffered(1))
    rows = lambda w: pl.BlockSpec((tm, w), lambda i: (i, 0))
    if mix is None:
        mix_args = [jnp.zeros((tm, w), BF16) for w in (D_CONV, D_RWKV, D_SB)] + [jnp.zeros((1, 16, d), BF16)]
        mix_specs = [pl.BlockSpec((tm, w), lambda i: (0, 0)) for w in (D_CONV, D_RWKV, D_SB)]
        wo_layer = 0
    else:
        mix_args = list(mix)
        mix_specs = [rows(D_CONV), rows(D_RWKV), rows(D_SB)]
        wo_layer = layer
    mix_specs.append(pl.BlockSpec((None,) + mix_args[3].shape[1:], lambda i: (wo_layer, 0, 0), **resident))
    fg = jnp.zeros((1, d), F32) if final_g is None else final_g
    return pl.pallas_call(
        functools.partial(_ffn_kernel, d_ff=d_ff, tf=tf, mix=mix is not None, final=final_g is not None),
        grid=(m // tm,),
        in_specs=[rows(d)] + mix_specs + [
            pl.BlockSpec((None, 1, d), lambda i: (layer, 0, 0)),
            pl.BlockSpec((None, d, 2 * d_ff), lambda i: (layer, 0, 0), **resident),
            pl.BlockSpec((None, d_ff, d), lambda i: (layer, 0, 0), **resident),
            pl.BlockSpec((1, d), lambda i: (0, 0)),
        ],
        out_specs=rows(d),
        out_shape=jax.ShapeDtypeStruct((m, d), F32),
        scratch_shapes=[pltpu.VMEM((tm, d_ff), BF16)],
        compiler_params=_cparams(1),
        name="ffn",
    )(h2d, *mix_args, norm_g, w13, w2, fg)


def _inproj_kernel(h_ref, g_ref, w_ref, pc_ref, pr_ref, ps_ref):
    xn = _rms_norm_rows(h_ref[...], g_ref[...]).astype(BF16)
    c0 = 2 * D_CONV
    c1 = c0 + D_RWKV_IN
    pc_ref[...] = jnp.dot(xn, w_ref[:, 0:c0], preferred_element_type=F32)
    pr_ref[...] = jnp.dot(xn, w_ref[:, c0:c1], preferred_element_type=F32)
    ps_ref[...] = jnp.dot(xn, w_ref[:, c1:], preferred_element_type=F32).astype(BF16)


def _inproj(h2d, norm_g, w_in, layer, *, tm=512):
    m, d = h2d.shape
    assert m % tm == 0
    d_in = w_in.shape[2]
    widths = (2 * D_CONV, D_RWKV_IN, 3 * D_SB)
    return pl.pallas_call(
        _inproj_kernel,
        grid=(m // tm,),
        in_specs=[
            pl.BlockSpec((tm, d), lambda i: (i, 0)),
            pl.BlockSpec((None, 1, d), lambda i: (layer, 0, 0)),
            pl.BlockSpec((None, d, d_in), lambda i: (layer, 0, 0)),
        ],
        out_specs=[pl.BlockSpec((tm, w), lambda i: (i, 0)) for w in widths],
        out_shape=[jax.ShapeDtypeStruct((m, w), dt) for w, dt in zip(widths, (F32, F32, BF16))],
        compiler_params=_cparams(1),
        name="inproj",
    )(h2d, norm_g, w_in)


CONV_PAD = 32
CONV_SUB = 64


def _conv_tile(e, p_ref, w_ref, b_ref, lg_ref, lb_ref, o_ref, buf_ref, shift_ref, tt):
    val = p_ref[e, :, 0:D_CONV]
    gate = p_ref[e, :, D_CONV:2 * D_CONV]
    buf_ref[e, CONV_PAD:CONV_PAD + tt, :] = val * _sigmoid(gate)
    first = CONV_PAD - (CONV_WIDTH - 1)
    for b in range(8):
        span = tt + 8 * ((CONV_WIDTH - 1 - b) // 8)
        shift_ref[e, b, 0:span, :] = buf_ref[e, first + b:first + b + span, :]
    for s in range(tt // CONV_SUB):
        acc = jnp.zeros((CONV_SUB, D_CONV), F32)
        for j in range(CONV_WIDTH):
            r0 = s * CONV_SUB + 8 * (j // 8)
            acc = acc + w_ref[j:j + 1, :] * shift_ref[e, j % 8, r0:r0 + CONV_SUB, :]
        y = acc + b_ref[...]
        mu = jnp.mean(y, axis=-1, keepdims=True)
        dlt = y - mu
        var = jnp.mean(dlt * dlt, axis=-1, keepdims=True)
        yn = dlt * lax.rsqrt(var + LN_EPS) * lg_ref[...] + lb_ref[...]
        o_ref[e, s * CONV_SUB:(s + 1) * CONV_SUB, :] = (yn * _sigmoid(yn)).astype(BF16)
    buf_ref[e, 0:CONV_PAD, :] = buf_ref[e, tt:tt + CONV_PAD, :]


def _rwkv_conv_kernel(p_ref, mu_ref, w0_ref, wb_ref, a0_ref, ab_ref, gb_ref, kk_ref, ka_ref, rk_ref, lg_ref, lb_ref,
                      bd_ref, tri_ref, pc_ref, cw_ref, cb_ref, clg_ref, clb_ref, o_ref, oc_ref,
                      st_ref, carry_ref, buf_ref, shift_ref, *, nb, tt):
    c = RWKV_CHUNK
    c2 = 2 * c
    n_pairs = D_RWKV // LANES
    chunks_per_seq = tt // c
    n_chunks = nb * chunks_per_seq
    t = pl.program_id(1)

    @pl.when(t == 0)
    def _():
        st_ref[...] = jnp.zeros_like(st_ref)
        carry_ref[...] = jnp.zeros_like(carry_ref)
        buf_ref[:, 0:CONV_PAD, :] = jnp.zeros((nb, CONV_PAD, D_CONV), F32)


    x = p_ref[...].reshape(nb * tt, D_RWKV_IN)
    row = lax.broadcasted_iota(jnp.int32, x.shape, 0)
    prev = pltpu.roll(x, 1, 0)
    for e in range(nb):
        prev = jnp.where(row == e * tt, carry_ref[e], prev)
        carry_ref[e] = x[(e + 1) * tt - 1:(e + 1) * tt, :]
    xs = x + mu_ref[...] * (prev - x)

    r = xs[:, 0:D_RWKV]
    k = xs[:, D_RWKV:2 * D_RWKV]
    v = xs[:, 2 * D_RWKV:3 * D_RWKV]
    lora_in = xs[:, 3 * D_RWKV:3 * D_RWKV + DECAY_LORA + AAA_LORA]
    gd = xs[:, 3 * D_RWKV + DECAY_LORA + AAA_LORA:]

    bd = bd_ref[...]

    def head_sum(z):
        hi, lo = _split2(z)
        return jnp.dot(hi, bd, preferred_element_type=F32) + jnp.dot(lo, bd, preferred_element_type=F32)

    w_log = -_softplus(-(w0_ref[...] + _dot(jnp.tanh(lora_in), wb_ref[...]))) - 0.5
    logw = -jnp.exp(w_log)
    a = _sigmoid(a0_ref[...] + _dot(lora_in, ab_ref[...]))
    g = _dot(_sigmoid(gd), gb_ref[...])
    kk = k * kk_ref[...]
    kk = kk / jnp.maximum(jnp.sqrt(head_sum(kk * kk)), 1e-12)
    kq = k * (1.0 + (a - 1.0) * ka_ref[...])
    na = -kk
    b = kk * a
    bonus = head_sum(r * kq * rk_ref[...]) * v

    tri = tri_ref[...]
    l1, l2, l3 = _split3(logw)
    cl = (jnp.dot(tri, l1, preferred_element_type=F32) + jnp.dot(tri, l2, preferred_element_type=F32)
          + jnp.dot(tri, l3, preferred_element_type=F32))
    cl_ends = [cl[(s + 1) * c - 1:(s + 1) * c, :] for s in range(n_chunks)]
    cl_end = jnp.concatenate([jnp.broadcast_to(e, (c, D_RWKV)) for e in cl_ends], axis=0)
    e_pos = jnp.exp(cl)
    e_neg = jnp.exp(-cl)
    e_end = jnp.exp(cl_end - cl)
    at_all = jnp.exp(cl - logw) * na
    bt_all = b * e_neg
    kt_all = kq * e_neg
    rt_all = r * e_pos
    bp_all = b * e_end
    kp_all = kq * e_end

    lane = lax.broadcasted_iota(jnp.int32, (c, LANES), 1)
    lane2 = lax.broadcasted_iota(jnp.int32, (c2, LANES), 1)
    m0 = lane < HEAD_DIM
    m0_2 = lane2 < HEAD_DIM
    ri = lax.broadcasted_iota(jnp.int32, (c2, c2), 0)
    cj = lax.broadcasted_iota(jnp.int32, (c2, c2), 1)
    ti = jnp.bitwise_and(ri, c - 1)
    sj = jnp.bitwise_and(cj, c - 1)
    mask_z = (ti > sj) | ((ri >= c) & (ti == sj))
    left_top = lax.broadcasted_iota(jnp.int32, (c, c2), 1) < c
    blockdiag = jnp.bitwise_and(ri, c) == jnp.bitwise_and(cj, c)
    eye = (ri == cj)
    zero = jnp.zeros((c, LANES), F32)
    rows = lambda parts: jnp.concatenate(parts, axis=0)
    cols = lambda parts: jnp.concatenate(parts, axis=1)

    units = [(s, j) for s in range(n_chunks) for j in range(n_pairs)]

    def tile(arr, u):
        s, j = u
        return arr[s * c:(s + 1) * c, j * LANES:(j + 1) * LANES]

    at = [tile(at_all, u) for u in units]
    rt = [tile(rt_all, u) for u in units]
    vv = [tile(v, u) for u in units]

    z0, z1 = [], []
    for i, u in enumerate(units):
        xx = rows([at[i], rt[i]])
        bt, kt = tile(bt_all, u), tile(kt_all, u)
        z0.append(jnp.where(mask_z, _dot_nt(jnp.where(m0_2, xx, 0.0), rows([bt, kt])), 0.0))
        z1.append(jnp.where(mask_z, _dot_nt(jnp.where(m0_2, 0.0, xx), rows([kt, bt])), 0.0))

    pw = [rows([jnp.where(left_top, z0[i][0:c], 0.0), jnp.where(left_top, 0.0, z1[i][0:c])]) for i in range(len(units))]
    tinv = [jnp.where(eye, 1.0, 0.0) + p for p in pw]
    for _ in range(5):
        pw = [_dot(p, p) for p in pw]
        tinv = [ti_ + _dot(ti_, p) for ti_, p in zip(tinv, pw)]

    akv = []
    for i in range(len(units)):
        v2 = rows([vv[i], vv[i]])
        akv.append(jnp.where(m0, _dot(jnp.where(left_top, 0.0, z0[i][0:c]), v2),
                             _dot(jnp.where(left_top, z1[i][0:c], 0.0), v2)))

    ah, uv = [], []
    for i in range(len(units)):
        wa = rows([jnp.where(m0, at[i], 0.0), jnp.where(m0, 0.0, at[i])])
        wv = rows([jnp.where(m0, akv[i], 0.0), jnp.where(m0, 0.0, akv[i])])
        tw = _dot(tinv[i], cols([wa, wv]))
        ah.append(tw[0:c, 0:LANES] + tw[c:c2, 0:LANES])
        uv.append(tw[0:c, LANES:] + tw[c:c2, LANES:])

    rh, ov, gmat, hmat = [], [], [], []
    for i, u in enumerate(units):
        rhs0 = cols([rows([ah[i], zero]), rows([uv[i], vv[i]])])
        rhs1 = cols([rows([zero, ah[i]]), rows([vv[i], uv[i]])])
        res0 = _dot(z0[i][c:c2], rhs0)
        res1 = _dot(z1[i][c:c2], rhs1)
        rh.append(rt[i] + jnp.where(m0, res0[:, 0:LANES], res1[:, 0:LANES]))
        ov.append(jnp.where(m0, res0[:, LANES:], res1[:, LANES:]))
        ypt = rows([tile(bp_all, u), tile(kp_all, u)]).T
        gh = _dot(ypt, rhs0)
        s, j = u
        p_end = jnp.exp(cl_ends[s][:, j * LANES:(j + 1) * LANES])
        gmat.append(jnp.where(blockdiag, gh[:, 0:LANES], 0.0)
                    + jnp.where(eye, jnp.broadcast_to(p_end, (c2, LANES)), 0.0))
        hmat.append(jnp.where(blockdiag, gh[:, LANES:], 0.0))

    states = [st_ref[n] for n in range(nb * n_pairs)]
    out_rows = []
    for s in range(n_chunks):
        out_pairs = []
        for j in range(n_pairs):
            i = s * n_pairs + j
            n = (s // chunks_per_seq) * n_pairs + j
            out_pairs.append(_dot(rh[i], states[n]) + ov[i])
            states[n] = _dot(gmat[i], states[n]) + hmat[i]
        out_rows.append(cols(out_pairs))
    for n in range(nb * n_pairs):
        st_ref[n] = states[n]

    o = rows(out_rows)
    mean = head_sum(o) * (1.0 / HEAD_DIM)
    dlt = o - mean
    var = head_sum(dlt * dlt) * (1.0 / HEAD_DIM)
    on = dlt * lax.rsqrt(var + GN_EPS) * lg_ref[...] + lb_ref[...]
    o_ref[...] = ((on + bonus) * g).astype(BF16).reshape(nb, tt, D_RWKV)

    for e in range(nb):
        _conv_tile(e, pc_ref, cw_ref, cb_ref, clg_ref, clb_ref, oc_ref, buf_ref, shift_ref, tt)


def _rwkv_conv_group(pr, pc, rwkv_params, conv_params, bd, layer, *, nb=2, tt=128):
    b, lp, _ = pr.shape
    assert b % nb == 0 and lp % tt == 0 and tt % RWKV_CHUNK == 0 and tt % CONV_SUB == 0
    n_pairs = D_RWKV // LANES
    idx = jnp.arange(nb * tt)
    tri = ((idx[:, None] >= idx[None, :]) & (idx[:, None] // RWKV_CHUNK == idx[None, :] // RWKV_CHUNK)).astype(BF16)

    def par(arr):
        return pl.BlockSpec((None,) + arr.shape[1:], lambda i, t: (layer,) + (0,) * (arr.ndim - 1))

    def const(arr):
        return pl.BlockSpec(arr.shape, lambda i, t: (0,) * arr.ndim)

    tile = lambda w: pl.BlockSpec((nb, tt, w), lambda i, t: (i, t, 0))
    return pl.pallas_call(
        functools.partial(_rwkv_conv_kernel, nb=nb, tt=tt),
        grid=(b // nb, lp // tt),
        in_specs=[tile(D_RWKV_IN)] + [par(p) for p in rwkv_params] + [const(bd), const(tri)]
        + [tile(2 * D_CONV)] + [par(p) for p in conv_params],
        out_specs=[tile(D_RWKV), tile(D_CONV)],
        out_shape=[jax.ShapeDtypeStruct((b, lp, D_RWKV), BF16), jax.ShapeDtypeStruct((b, lp, D_CONV), BF16)],
        scratch_shapes=[pltpu.VMEM((nb * n_pairs, LANES, LANES), F32), pltpu.VMEM((nb, 1, D_RWKV_IN), F32),
                        pltpu.VMEM((nb, tt + CONV_PAD, D_CONV), F32),
                        pltpu.VMEM((nb, 8, tt + 8 * ((CONV_WIDTH - 1) // 8), D_CONV), F32)],
        compiler_params=_cparams(2),
        name="rwkv_conv_group",
    )(pr, *rwkv_params, bd, tri, pc, *conv_params)


SB_TILE = 128


def _sb_kernel(q_ref, k_ref, v_ref, g_ref, bd_ref, later_ref, o_ref, km_ref, vm_ref, *, nb):
    t = SB_TILE
    n_pairs = D_SB // LANES
    n_streams = nb * n_pairs
    qi = pl.program_id(1)

    @pl.when(qi == 0)
    def _():
        m0 = lax.broadcasted_iota(jnp.int32, (k_ref.shape[1], LANES), 1) < HEAD_DIM
        for p in range(n_streams):
            e, j = divmod(p, n_pairs)
            kj = k_ref[e, :, j * LANES:(j + 1) * LANES]
            vj = v_ref[e, :, j * LANES:(j + 1) * LANES]
            zero = jnp.zeros_like(kj)
            km_ref[2 * p] = jnp.where(m0, kj, zero)
            km_ref[2 * p + 1] = jnp.where(m0, zero, kj)
            vm_ref[2 * p] = jnp.where(m0, vj, zero)
            vm_ref[2 * p + 1] = jnp.where(m0, zero, vj)

    rowi = lax.broadcasted_iota(jnp.int32, (t, t), 0)
    coli = lax.broadcasted_iota(jnp.int32, (t, t), 1)
    causal = coli < rowi
    causal2 = jnp.concatenate([causal, causal], axis=1)
    later2 = later_ref[...]
    qb = [q_ref[p // n_pairs, :, (p % n_pairs) * LANES:(p % n_pairs + 1) * LANES] * (HEAD_DIM ** -0.5)
          for p in range(n_streams)]

    def both_heads(ref, kb, p):
        off = kb * t if isinstance(kb, int) else pl.multiple_of(kb * t, t)
        return jnp.concatenate([ref[2 * p, pl.ds(off, t), :], ref[2 * p + 1, pl.ds(off, t), :]], axis=0)

    def scores(kbs):
        return tuple(lax.dot_general(qb[p], both_heads(km_ref, kb, p), (((1,), (1,)), ((), ())),
                                     preferred_element_type=F32) for kb in kbs for p in range(n_streams))

    def group(kbs, zs, accs, carries, diagonal):
        units = [(kb, p) for kb in kbs for p in range(n_streams)]
        sps = [_softplus(z.astype(BF16)) for z in zs]
        if diagonal:
            sps = [jnp.where(causal2, sp, jnp.zeros_like(sp)) for sp in sps]
        locs = [jnp.dot(sp, later2, preferred_element_type=F32) for sp in sps]
        accs, carries = list(accs), list(carries)
        for i, (kb, p) in enumerate(units):
            halves = []
            for m in range(2):
                sl = slice(m * t, (m + 1) * t)
                att = jnp.exp(zs[i][:, sl] - locs[i][:, sl] - carries[2 * p + m])
                if diagonal:
                    att = jnp.where(causal, att, 0.0)
                halves.append(att.astype(BF16))
                carries[2 * p + m] = carries[2 * p + m] + locs[i][:, m * t:m * t + 1]
            accs[p] = accs[p] + jnp.dot(jnp.concatenate(halves, axis=1), both_heads(vm_ref, kb, p),
                                        preferred_element_type=F32)
        return tuple(accs), tuple(carries)

    state = (tuple(jnp.zeros((t, LANES), F32) for _ in range(n_streams)),
             tuple(jnp.zeros((t, 1), F32) for _ in range(2 * n_streams)))
    state = group([qi], scores([qi]), *state, True)

    def run(first, n_iters, width, state):
        def body(it, c):
            kbs = [first - width * it - w for w in range(width)]
            zs_next = scores(kbs[0:1])
            for w in range(width):
                zs = zs_next
                if w + 1 < width:
                    zs_next = scores(kbs[w + 1:w + 2])
                c = group(kbs[w:w + 1], zs, *c, False)
            return c
        return lax.fori_loop(0, n_iters, body, state)

    n4 = jnp.right_shift(qi, 2)
    n2 = jnp.bitwise_and(jnp.right_shift(qi, 1), 1)
    state = run(qi - 1, n4, 4, state)
    state = run(qi - 1 - 4 * n4, n2, 2, state)
    state = run(0, jnp.bitwise_and(qi, 1), 1, state)
    accs = state[0]

    for p in range(n_streams):
        e, j = divmod(p, n_pairs)
        o = accs[p]
        hi, lo = _split2(o * o)
        ss = jnp.dot(hi, bd_ref[...], preferred_element_type=F32) + jnp.dot(lo, bd_ref[...], preferred_element_type=F32)
        o_ref[e, :, j * LANES:(j + 1) * LANES] = (o * lax.rsqrt(ss * (1.0 / HEAD_DIM) + RMS_EPS)
                                                  * g_ref[:, j * LANES:(j + 1) * LANES]).astype(BF16)


def _sb_group(ps, norm_g, bd128, layer, *, nb=2):
    b, lp, _ = ps.shape
    assert lp % SB_TILE == 0 and b % nb == 0
    n_pairs = D_SB // LANES
    idx = jnp.arange(2 * SB_TILE)
    later2 = ((idx[:, None] >= idx[None, :]) & (idx[:, None] // SB_TILE == idx[None, :] // SB_TILE)).astype(BF16)
    masked_copies = pltpu.VMEM((2 * nb * n_pairs, lp, LANES), BF16)
    return pl.pallas_call(
        functools.partial(_sb_kernel, nb=nb),
        grid=(b // nb, lp // SB_TILE),
        in_specs=[
            pl.BlockSpec((nb, SB_TILE, D_SB), lambda i, q: (i, q, 0)),
            pl.BlockSpec((nb, lp, D_SB), lambda i, q: (i, 0, 1)),
            pl.BlockSpec((nb, lp, D_SB), lambda i, q: (i, 0, 2)),
            pl.BlockSpec((None, 1, D_SB), lambda i, q: (layer, 0, 0)),
            pl.BlockSpec((LANES, LANES), lambda i, q: (0, 0)),
            pl.BlockSpec(later2.shape, lambda i, q: (0, 0)),
        ],
        out_specs=pl.BlockSpec((nb, SB_TILE, D_SB), lambda i, q: (i, q, 0)),
        out_shape=jax.ShapeDtypeStruct((b, lp, D_SB), BF16),
        scratch_shapes=[masked_copies, masked_copies],
        compiler_params=_cparams(2),
        name="sb_group",
    )(ps, ps, ps, norm_g, bd128, later2)


def _block_diag_ones(n):
    i = jnp.arange(n) // HEAD_DIM
    return (i[:, None] == i[None, :]).astype(BF16)


@jax.jit
def _trunk(x, meta, ffn1_norm, ffn1_w13, ffn1_w2, mix_norm, w_in, conv_w, conv_b, conv_ln_g, conv_ln_b, rwkv_mu,
           rwkv_w0, rwkv_wB, rwkv_a0, rwkv_aB, rwkv_gB, rwkv_kk, rwkv_ka, rwkv_rk, rwkv_ln_g, rwkv_ln_b, sb_norm,
           w_out, ffn2_norm, ffn2_w13, ffn2_w2, final_norm):
    bsz, seq, d = x.shape
    depth = w_in.shape[0]
    l_real = N_META + seq
    lp = -(-l_real // SEQ_ALIGN) * SEQ_ALIGN
    meta_b = jnp.broadcast_to(meta.astype(x.dtype)[None], (bsz, N_META, d))
    h = jnp.concatenate([meta_b, x, jnp.zeros((bsz, lp - l_real, d), x.dtype)], axis=1).reshape(bsz * lp, d)

    row3 = lambda p: p.reshape(depth, 1, -1)
    ffn1_w13, ffn1_w2, ffn2_w13, ffn2_w2 = (w.astype(BF16) for w in (ffn1_w13, ffn1_w2, ffn2_w13, ffn2_w2))
    w_in, w_out = w_in.astype(BF16), w_out.astype(BF16)
    ffn1_norm, mix_norm, ffn2_norm, sb_norm = row3(ffn1_norm), row3(mix_norm), row3(ffn2_norm), row3(sb_norm)
    conv_b, conv_ln_g, conv_ln_b = row3(conv_b), row3(conv_ln_g), row3(conv_ln_b)
    rwkv_vecs = [row3(p) for p in (rwkv_mu, rwkv_w0, rwkv_a0, rwkv_kk, rwkv_ka, rwkv_rk, rwkv_ln_g, rwkv_ln_b)]
    mu, w0, a0, k_k, k_a, r_k, ln_g, ln_b = rwkv_vecs
    wb_ext = jnp.pad(rwkv_wB, ((0, 0), (0, AAA_LORA), (0, 0))).astype(BF16)
    ab_ext = jnp.pad(rwkv_aB, ((0, 0), (DECAY_LORA, 0), (0, 0))).astype(BF16)
    gb = rwkv_gB.astype(BF16)
    bd_rwkv = _block_diag_ones(D_RWKV)
    bd_pair = _block_diag_ones(LANES)

    for l in range(depth):
        h = _ffn(h, ffn1_norm, ffn1_w13, ffn1_w2, l)
        pc, pr, ps = _inproj(h, mix_norm, w_in, l)
        yr, yc = _rwkv_conv_group(pr.reshape(bsz, lp, -1), pc.reshape(bsz, lp, -1),
                                  (mu, w0, wb_ext, a0, ab_ext, gb, k_k, k_a, r_k, ln_g, ln_b),
                                  (conv_w, conv_b, conv_ln_g, conv_ln_b), bd_rwkv, l)
        ys = _sb_group(ps.reshape(bsz, lp, -1), sb_norm, bd_pair, l)
        mix = (yc.reshape(bsz * lp, -1), yr.reshape(bsz * lp, -1), ys.reshape(bsz * lp, -1), w_out)
        h = _ffn(h, ffn2_norm, ffn2_w13, ffn2_w2, l, mix=mix,
                 final_g=final_norm.reshape(1, d) if l == depth - 1 else None)
    out = h
    return out.reshape(bsz, lp, d)[:, N_META:l_real]


def kernel(x, meta, ffn1_norm, ffn1_w13, ffn1_w2, mix_norm, w_in, conv_w, conv_b, conv_ln_g, conv_ln_b, rwkv_mu,
           rwkv_w0, rwkv_wB, rwkv_a0, rwkv_aB, rwkv_gB, rwkv_kk, rwkv_ka, rwkv_rk, rwkv_ln_g, rwkv_ln_b, sb_norm,
           w_out, ffn2_norm, ffn2_w13, ffn2_w2, final_norm):
    return _trunk(x, meta, ffn1_norm, ffn1_w13, ffn1_w2, mix_norm, w_in, conv_w, conv_b, conv_ln_g, conv_ln_b,
                  rwkv_mu, rwkv_w0, rwkv_wB, rwkv_a0, rwkv_aB, rwkv_gB, rwkv_kk, rwkv_ka, rwkv_rk, rwkv_ln_g,
                  rwkv_ln_b, sb_norm, w_out, ffn2_norm, ffn2_w13, ffn2_w2, final_norm)
```

```python
import functools

import jax
import jax.numpy as jnp
from jax import lax
from jax.experimental import pallas as pl
from jax.experimental.pallas import tpu as pltpu

F32 = jnp.float32
BF16 = jnp.bfloat16

N_META = 16
HEAD_DIM = 64
D_CONV = 256
CONV_WIDTH = 31
D_RWKV = 384
D_SB = 384
DECAY_LORA = 64
AAA_LORA = 64
GATE_LORA = 128
D_RWKV_IN = 3 * D_RWKV + DECAY_LORA + AAA_LORA + GATE_LORA
RMS_EPS = 1e-6
LN_EPS = 1e-5
GN_EPS = 64e-5

LANES = 128
SEQ_ALIGN = 128
RWKV_CHUNK = 64
VMEM_LIMIT = 56 * 1024 * 1024


def _cparams(n_axes):
    return pltpu.CompilerParams(dimension_semantics=("arbitrary",) * n_axes, vmem_limit_bytes=VMEM_LIMIT)


def _dot(a, b):
    return jnp.dot(a.astype(BF16), b.astype(BF16), preferred_element_type=F32)


def _dot_nt(a, b):
    return lax.dot_general(a.astype(BF16), b.astype(BF16), (((1,), (1,)), ((), ())), preferred_element_type=F32)


def _split3(x):
    h1 = x.astype(BF16)
    r1 = x - h1.astype(F32)
    h2 = r1.astype(BF16)
    return h1, h2, (r1 - h2.astype(F32)).astype(BF16)


def _split2(x):
    hi = x.astype(BF16)
    return hi, (x - hi.astype(F32)).astype(BF16)


def _sigmoid(x):
    return 0.5 * jnp.tanh(0.5 * x) + 0.5


def _softplus(x):
    return jnp.maximum(x, 0.0) + jnp.log(1.0 + jnp.exp(-jnp.abs(x)))


def _rms_norm_rows(x, g):
    ms = jnp.mean(x * x, axis=-1, keepdims=True)
    return x * lax.rsqrt(ms + RMS_EPS) * g


def _ffn_kernel(h_ref, yc_ref, yr_ref, ys_ref, wo_ref, g_ref, w13_ref, w2_ref, fg_ref, o_ref, act_ref, *,
                d_ff, tf, mix, final):
    hin = h_ref[...]
    if mix:
        r0 = D_CONV
        r1 = D_CONV + D_RWKV
        hin = hin + jnp.dot(yc_ref[...], wo_ref[0:r0, :], preferred_element_type=F32)
        hin = hin + jnp.dot(yr_ref[...], wo_ref[r0:r1, :], preferred_element_type=F32)
        hin = hin + jnp.dot(ys_ref[...], wo_ref[r1:, :], preferred_element_type=F32)
    o_ref[...] = hin
    xn = _rms_norm_rows(hin, g_ref[...]).astype(BF16)
    for c0 in range(0, d_ff, tf):
        gate = jnp.dot(xn, w13_ref[:, c0:c0 + tf], preferred_element_type=F32)
        up = jnp.dot(xn, w13_ref[:, d_ff + c0:d_ff + c0 + tf], preferred_element_type=F32)
        act_ref[:, c0:c0 + tf] = (gate * _sigmoid(gate) * up).astype(BF16)
    out = o_ref[...] + 0.5 * jnp.dot(act_ref[...], w2_ref[...], preferred_element_type=F32)
    if final:
        out = _rms_norm_rows(out, fg_ref[...])
    o_ref[...] = out


def _ffn(h2d, norm_g, w13, w2, layer, *, mix=None, final_g=None, tm=512, tf=256):
    m, d = h2d.shape
    d_ff = w2.shape[1]
    assert m % tm == 0 and d_ff % tf == 0
    resident = dict(pipeline_mode=pl.Buffered(1))
    rows = lambda w: pl.BlockSpec((tm, w), lambda i: (i, 0))
    if mix is None:
        mix_args = [jnp.zeros((tm, w), BF16) for w in (D_CONV, D_RWKV, D_SB)] + [jnp.zeros((1, 16, d), BF16)]
        mix_specs = [pl.BlockSpec((tm, w), lambda i: (0, 0)) for w in (D_CONV, D_RWKV, D_SB)]
        wo_layer = 0
    else:
        mix_args = list(mix)
        mix_specs = [rows(D_CONV), rows(D_RWKV), rows(D_SB)]
        wo_layer = layer
    mix_specs.append(pl.BlockSpec((None,) + mix_args[3].shape[1:], lambda i: (wo_layer, 0, 0), **resident))
    fg = jnp.zeros((1, d), F32) if final_g is None else final_g
    return pl.pallas_call(
        functools.partial(_ffn_kernel, d_ff=d_ff, tf=tf, mix=mix is not None, final=final_g is not None),
        grid=(m // tm,),
        in_specs=[rows(d)] + mix_specs + [
            pl.BlockSpec((None, 1, d), lambda i: (layer, 0, 0)),
            pl.BlockSpec((None, d, 2 * d_ff), lambda i: (layer, 0, 0), **resident),
            pl.BlockSpec((None, d_ff, d), lambda i: (layer, 0, 0), **resident),
            pl.BlockSpec((1, d), lambda i: (0, 0)),
        ],
        out_specs=rows(d),
        out_shape=jax.ShapeDtypeStruct((m, d), F32),
        scratch_shapes=[pltpu.VMEM((tm, d_ff), BF16)],
        compiler_params=_cparams(1),
        name="ffn",
    )(h2d, *mix_args, norm_g, w13, w2, fg)


def _inproj_kernel(h_ref, g_ref, w_ref, pc_ref, pr_ref, ps_ref):
    xn = _rms_norm_rows(h_ref[...], g_ref[...]).astype(BF16)
    c0 = 2 * D_CONV
    c1 = c0 + D_RWKV_IN
    pc_ref[...] = jnp.dot(xn, w_ref[:, 0:c0], preferred_element_type=F32)
    pr_ref[...] = jnp.dot(xn, w_ref[:, c0:c1], preferred_element_type=F32)
    ps_ref[...] = jnp.dot(xn, w_ref[:, c1:], preferred_element_type=F32).astype(BF16)


def _inproj(h2d, norm_g, w_in, layer, *, tm=512):
    m, d = h2d.shape
    assert m % tm == 0
    d_in = w_in.shape[2]
    widths = (2 * D_CONV, D_RWKV_IN, 3 * D_SB)
    return pl.pallas_call(
        _inproj_kernel,
        grid=(m // tm,),
        in_specs=[
            pl.BlockSpec((tm, d), lambda i: (i, 0)),
            pl.BlockSpec((None, 1, d), lambda i: (layer, 0, 0)),
            pl.BlockSpec((None, d, d_in), lambda i: (layer, 0, 0)),
        ],
        out_specs=[pl.BlockSpec((tm, w), lambda i: (i, 0)) for w in widths],
        out_shape=[jax.ShapeDtypeStruct((m, w), dt) for w, dt in zip(widths, (F32, F32, BF16))],
        compiler_params=_cparams(1),
        name="inproj",
    )(h2d, norm_g, w_in)


CONV_PAD = 32
CONV_SUB = 64


def _conv_tile(e, p_ref, w_ref, b_ref, lg_ref, lb_ref, o_ref, buf_ref, shift_ref, tt):
    val = p_ref[e, :, 0:D_CONV]
    gate = p_ref[e, :, D_CONV:2 * D_CONV]
    buf_ref[e, CONV_PAD:CONV_PAD + tt, :] = val * _sigmoid(gate)
    first = CONV_PAD - (CONV_WIDTH - 1)
    for b in range(8):
        span = tt + 8 * ((CONV_WIDTH - 1 - b) // 8)
        shift_ref[e, b, 0:span, :] = buf_ref[e, first + b:first + b + span, :]
    for s in range(tt // CONV_SUB):
        acc = jnp.zeros((CONV_SUB, D_CONV), F32)
        for j in range(CONV_WIDTH):
            r0 = s * CONV_SUB + 8 * (j // 8)
            acc = acc + w_ref[j:j + 1, :] * shift_ref[e, j % 8, r0:r0 + CONV_SUB, :]
        y = acc + b_ref[...]
        mu = jnp.mean(y, axis=-1, keepdims=True)
        dlt = y - mu
        var = jnp.mean(dlt * dlt, axis=-1, keepdims=True)
        yn = dlt * lax.rsqrt(var + LN_EPS) * lg_ref[...] + lb_ref[...]
        o_ref[e, s * CONV_SUB:(s + 1) * CONV_SUB, :] = (yn * _sigmoid(yn)).astype(BF16)
    buf_ref[e, 0:CONV_PAD, :] = buf_ref[e, tt:tt + CONV_PAD, :]


def _rwkv_conv_kernel(p_ref, mu_ref, w0_ref, wb_ref, a0_ref, ab_ref, gb_ref, kk_ref, ka_ref, rk_ref, lg_ref, lb_ref,
                      bd_ref, tri_ref, pc_ref, cw_ref, cb_ref, clg_ref, clb_ref, o_ref, oc_ref,
                      st_ref, carry_ref, buf_ref, shift_ref, *, nb, tt):
    c = RWKV_CHUNK
    c2 = 2 * c
    n_pairs = D_RWKV // LANES
    chunks_per_seq = tt // c
    n_chunks = nb * chunks_per_seq
    t = pl.program_id(1)

    @pl.when(t == 0)
    def _():
        st_ref[...] = jnp.zeros_like(st_ref)
        carry_ref[...] = jnp.zeros_like(carry_ref)
        buf_ref[:, 0:CONV_PAD, :] = jnp.zeros((nb, CONV_PAD, D_CONV), F32)


    x = p_ref[...].reshape(nb * tt, D_RWKV_IN)
    row = lax.broadcasted_iota(jnp.int32, x.shape, 0)
    prev = pltpu.roll(x, 1, 0)
    for e in range(nb):
        prev = jnp.where(row == e * tt, carry_ref[e], prev)
        carry_ref[e] = x[(e + 1) * tt - 1:(e + 1) * tt, :]
    xs = x + mu_ref[...] * (prev - x)

    r = xs[:, 0:D_RWKV]
    k = xs[:, D_RWKV:2 * D_RWKV]
    v = xs[:, 2 * D_RWKV:3 * D_RWKV]
    lora_in = xs[:, 3 * D_RWKV:3 * D_RWKV + DECAY_LORA + AAA_LORA]
    gd = xs[:, 3 * D_RWKV + DECAY_LORA + AAA_LORA:]

    bd = bd_ref[...]

    def head_sum(z):
        hi, lo = _split2(z)
        return jnp.dot(hi, bd, preferred_element_type=F32) + jnp.dot(lo, bd, preferred_element_type=F32)

    w_log = -_softplus(-(w0_ref[...] + _dot(jnp.tanh(lora_in), wb_ref[...]))) - 0.5
    logw = -jnp.exp(w_log)
    a = _sigmoid(a0_ref[...] + _dot(lora_in, ab_ref[...]))
    g = _dot(_sigmoid(gd), gb_ref[...])
    kk = k * kk_ref[...]
    kk = kk * lax.rsqrt(jnp.maximum(head_sum(kk * kk), 1e-24))
    kq = k * (1.0 + (a - 1.0) * ka_ref[...])
    na = -kk
    b = kk * a
    bonus = head_sum(r * kq * rk_ref[...]) * v

    tri = tri_ref[...]
    l1, l2, l3 = _split3(logw)
    cl = (jnp.dot(tri, l1, preferred_element_type=F32) + jnp.dot(tri, l2, preferred_element_type=F32)
          + jnp.dot(tri, l3, preferred_element_type=F32))
    cl_ends = [cl[(s + 1) * c - 1:(s + 1) * c, :] for s in range(n_chunks)]
    cl_end = jnp.concatenate([jnp.broadcast_to(e, (c, D_RWKV)) for e in cl_ends], axis=0)
    e_pos = jnp.exp(cl)
    e_neg = jnp.exp(-cl)
    e_end = jnp.exp(cl_end - cl)
    at_all = jnp.exp(cl - logw) * na
    bt_all = b * e_neg
    kt_all = kq * e_neg
    rt_all = r * e_pos
    bp_all = b * e_end
    kp_all = kq * e_end

    lane = lax.broadcasted_iota(jnp.int32, (c, LANES), 1)
    lane2 = lax.broadcasted_iota(jnp.int32, (c2, LANES), 1)
    m0 = lane < HEAD_DIM
    m0_2 = lane2 < HEAD_DIM
    ri = lax.broadcasted_iota(jnp.int32, (c2, c2), 0)
    cj = lax.broadcasted_iota(jnp.int32, (c2, c2), 1)
    ti = jnp.bitwise_and(ri, c - 1)
    sj = jnp.bitwise_and(cj, c - 1)
    mask_z = (ti > sj) | ((ri >= c) & (ti == sj))
    left_top = lax.broadcasted_iota(jnp.int32, (c, c2), 1) < c
    blockdiag = jnp.bitwise_and(ri, c) == jnp.bitwise_and(cj, c)
    eye = (ri == cj)
    zero = jnp.zeros((c, LANES), F32)
    rows = lambda parts: jnp.concatenate(parts, axis=0)
    cols = lambda parts: jnp.concatenate(parts, axis=1)

    units = [(s, j) for s in range(n_chunks) for j in range(n_pairs)]

    def tile(arr, u):
        s, j = u
        return arr[s * c:(s + 1) * c, j * LANES:(j + 1) * LANES]

    at = [tile(at_all, u) for u in units]
    rt = [tile(rt_all, u) for u in units]
    vv = [tile(v, u) for u in units]

    z0, z1 = [], []
    for i, u in enumerate(units):
        xx = rows([at[i], rt[i]])
        bt, kt = tile(bt_all, u), tile(kt_all, u)
        z0.append(jnp.where(mask_z, _dot_nt(jnp.where(m0_2, xx, 0.0), rows([bt, kt])), 0.0))
        z1.append(jnp.where(mask_z, _dot_nt(jnp.where(m0_2, 0.0, xx), rows([kt, bt])), 0.0))

    pw = [rows([jnp.where(left_top, z0[i][0:c], 0.0), jnp.where(left_top, 0.0, z1[i][0:c])]) for i in range(len(units))]
    identity = jnp.where(eye, 1.0, 0.0)
    tinv = [identity + p for p in pw]
    for _ in range(5):
        pw = [_dot(p, p) for p in pw]
        tinv = [ti_ + _dot(ti_, p) for ti_, p in zip(tinv, pw)]

    akv = []
    for i in range(len(units)):
        v2 = rows([vv[i], vv[i]])
        akv.append(jnp.where(m0, _dot(jnp.where(left_top, 0.0, z0[i][0:c]), v2),
                             _dot(jnp.where(left_top, z1[i][0:c], 0.0), v2)))

    ah, uv = [], []
    for i in range(len(units)):
        wa = rows([jnp.where(m0, at[i], 0.0), jnp.where(m0, 0.0, at[i])])
        wv = rows([jnp.where(m0, akv[i], 0.0), jnp.where(m0, 0.0, akv[i])])
        tw = _dot(tinv[i], cols([wa, wv]))
        ah.append(tw[0:c, 0:LANES] + tw[c:c2, 0:LANES])
        uv.append(tw[0:c, LANES:] + tw[c:c2, LANES:])

    rh, ov, gmat, hmat = [], [], [], []
    for i, u in enumerate(units):
        rhs0 = cols([rows([ah[i], zero]), rows([uv[i], vv[i]])])
        rhs1 = cols([rows([zero, ah[i]]), rows([vv[i], uv[i]])])
        res0 = _dot(z0[i][c:c2], rhs0)
        res1 = _dot(z1[i][c:c2], rhs1)
        rh.append(rt[i] + jnp.where(m0, res0[:, 0:LANES], res1[:, 0:LANES]))
        ov.append(jnp.where(m0, res0[:, LANES:], res1[:, LANES:]))
        ypt = rows([tile(bp_all, u), tile(kp_all, u)]).T
        gh = _dot(ypt, rhs0)
        s, j = u
        p_end = jnp.exp(cl_ends[s][:, j * LANES:(j + 1) * LANES])
        gmat.append(jnp.where(blockdiag, gh[:, 0:LANES], 0.0)
                    + jnp.where(eye, jnp.broadcast_to(p_end, (c2, LANES)), 0.0))
        hmat.append(jnp.where(blockdiag, gh[:, LANES:], 0.0))

    states = [st_ref[n] for n in range(nb * n_pairs)]
    out_rows = []
    for s in range(n_chunks):
        out_pairs = []
        for j in range(n_pairs):
            i = s * n_pairs + j
            n = (s // chunks_per_seq) * n_pairs + j
            out_pairs.append(_dot(rh[i], states[n]) + ov[i])
            states[n] = _dot(gmat[i], states[n]) + hmat[i]
        out_rows.append(cols(out_pairs))
    for n in range(nb * n_pairs):
        st_ref[n] = states[n]

    o = rows(out_rows)
    mean = head_sum(o) * (1.0 / HEAD_DIM)
    dlt = o - mean
    var = head_sum(dlt * dlt) * (1.0 / HEAD_DIM)
    on = dlt * lax.rsqrt(var + GN_EPS) * lg_ref[...] + lb_ref[...]
    o_ref[...] = ((on + bonus) * g).astype(BF16).reshape(nb, tt, D_RWKV)

    for e in range(nb):
        _conv_tile(e, pc_ref, cw_ref, cb_ref, clg_ref, clb_ref, oc_ref, buf_ref, shift_ref, tt)


def _rwkv_conv_group(pr, pc, rwkv_params, conv_params, bd, layer, *, nb=4, tt=64):
    b, lp, _ = pr.shape
    assert b % nb == 0 and lp % tt == 0 and tt % RWKV_CHUNK == 0 and tt % CONV_SUB == 0
    n_pairs = D_RWKV // LANES
    idx = jnp.arange(nb * tt)
    tri = ((idx[:, None] >= idx[None, :]) & (idx[:, None] // RWKV_CHUNK == idx[None, :] // RWKV_CHUNK)).astype(BF16)

    def par(arr):
        return pl.BlockSpec((None,) + arr.shape[1:], lambda i, t: (layer,) + (0,) * (arr.ndim - 1))

    def const(arr):
        return pl.BlockSpec(arr.shape, lambda i, t: (0,) * arr.ndim)

    tile = lambda w: pl.BlockSpec((nb, tt, w), lambda i, t: (i, t, 0))
    return pl.pallas_call(
        functools.partial(_rwkv_conv_kernel, nb=nb, tt=tt),
        grid=(b // nb, lp // tt),
        in_specs=[tile(D_RWKV_IN)] + [par(p) for p in rwkv_params] + [const(bd), const(tri)]
        + [tile(2 * D_CONV)] + [par(p) for p in conv_params],
        out_specs=[tile(D_RWKV), tile(D_CONV)],
        out_shape=[jax.ShapeDtypeStruct((b, lp, D_RWKV), BF16), jax.ShapeDtypeStruct((b, lp, D_CONV), BF16)],
        scratch_shapes=[pltpu.VMEM((nb * n_pairs, LANES, LANES), F32), pltpu.VMEM((nb, 1, D_RWKV_IN), F32),
                        pltpu.VMEM((nb, tt + CONV_PAD, D_CONV), F32),
                        pltpu.VMEM((nb, 8, tt + 8 * ((CONV_WIDTH - 1) // 8), D_CONV), F32)],
        compiler_params=_cparams(2),
        name="rwkv_conv_group",
    )(pr, *rwkv_params, bd, tri, pc, *conv_params)


SB_TILE = 128
SB_GROUP_TILES = 8


def _sb_kernel(q_ref, k_ref, v_ref, g_ref, bd_ref, later_ref, o_ref, km_ref, vm_ref, *, nb):
    t = SB_TILE
    n_pairs = D_SB // LANES
    n_streams = nb * n_pairs
    qi = pl.program_id(1)

    @pl.when(qi == 0)
    def _():
        m0 = lax.broadcasted_iota(jnp.int32, (k_ref.shape[1], LANES), 1) < HEAD_DIM
        for p in range(n_streams):
            e, j = divmod(p, n_pairs)
            kj = k_ref[e, :, j * LANES:(j + 1) * LANES]
            vj = v_ref[e, :, j * LANES:(j + 1) * LANES]
            zero = jnp.zeros_like(kj)
            km_ref[2 * p] = jnp.where(m0, kj, zero)
            km_ref[2 * p + 1] = jnp.where(m0, zero, kj)
            vm_ref[2 * p] = jnp.where(m0, vj, zero)
            vm_ref[2 * p + 1] = jnp.where(m0, zero, vj)

    rowi = lax.broadcasted_iota(jnp.int32, (t, t), 0)
    coli = lax.broadcasted_iota(jnp.int32, (t, t), 1)
    causal = coli < rowi
    causal2 = jnp.concatenate([causal, causal], axis=1)
    later2 = later_ref[...]
    qb = [q_ref[p // n_pairs, :, (p % n_pairs) * LANES:(p % n_pairs + 1) * LANES] * (HEAD_DIM ** -0.5)
          for p in range(n_streams)]

    def both_heads(ref, kb, p):
        off = kb * t if isinstance(kb, int) else pl.multiple_of(kb * t, t)
        return jnp.concatenate([ref[2 * p, pl.ds(off, t), :], ref[2 * p + 1, pl.ds(off, t), :]], axis=0)

    def scores(kbs):
        return tuple(lax.dot_general(qb[p], both_heads(km_ref, kb, p), (((1,), (1,)), ((), ())),
                                     preferred_element_type=F32) for kb in kbs for p in range(n_streams))

    def group(kbs, zs, accs, carries, diagonal):
        units = [(kb, p) for kb in kbs for p in range(n_streams)]
        sps = [_softplus(z.astype(BF16)) for z in zs]
        if diagonal:
            sps = [jnp.where(causal2, sp, jnp.zeros_like(sp)) for sp in sps]
        locs = [jnp.dot(sp, later2, preferred_element_type=F32) for sp in sps]
        accs, carries = list(accs), list(carries)
        for i, (kb, p) in enumerate(units):
            halves = []
            for m in range(2):
                sl = slice(m * t, (m + 1) * t)
                att = jnp.exp(zs[i][:, sl] - locs[i][:, sl] - carries[2 * p + m])
                if diagonal:
                    att = jnp.where(causal, att, 0.0)
                halves.append(att.astype(BF16))
                carries[2 * p + m] = carries[2 * p + m] + locs[i][:, m * t:m * t + 1]
            accs[p] = accs[p] + jnp.dot(jnp.concatenate(halves, axis=1), both_heads(vm_ref, kb, p),
                                        preferred_element_type=F32)
        return tuple(accs), tuple(carries)

    state = (tuple(jnp.zeros((t, LANES), F32) for _ in range(n_streams)),
             tuple(jnp.zeros((t, 1), F32) for _ in range(2 * n_streams)))
    state = group([qi], scores([qi]), *state, True)

    def run(first, n_iters, width, state):
        def body(it, c):
            kbs = [first - width * it - w for w in range(width)]
            zs_next = scores(kbs[0:1])
            for w in range(width):
                zs = zs_next
                if w + 1 < width:
                    zs_next = scores(kbs[w + 1:w + 2])
                c = group(kbs[w:w + 1], zs, *c, False)
            return c
        return lax.fori_loop(0, n_iters, body, state)

    width = max(1, SB_GROUP_TILES // nb)
    n = jnp.right_shift(qi, width.bit_length() - 1)
    state = run(qi - 1, n, width, state)
    done = n * width
    while width > 1:
        width //= 2
        n = jnp.bitwise_and(jnp.right_shift(qi, width.bit_length() - 1), 1)
        state = run(qi - 1 - done, n, width, state)
        done = done + n * width
    accs = state[0]

    for p in range(n_streams):
        e, j = divmod(p, n_pairs)
        o = accs[p]
        hi, lo = _split2(o * o)
        ss = jnp.dot(hi, bd_ref[...], preferred_element_type=F32) + jnp.dot(lo, bd_ref[...], preferred_element_type=F32)
        o_ref[e, :, j * LANES:(j + 1) * LANES] = (o * lax.rsqrt(ss * (1.0 / HEAD_DIM) + RMS_EPS)
                                                  * g_ref[:, j * LANES:(j + 1) * LANES]).astype(BF16)


def _sb_group(ps, norm_g, bd128, layer, *, nb=2):
    b, lp, _ = ps.shape
    assert lp % SB_TILE == 0 and b % nb == 0
    n_pairs = D_SB // LANES
    idx = jnp.arange(2 * SB_TILE)
    later2 = ((idx[:, None] >= idx[None, :]) & (idx[:, None] // SB_TILE == idx[None, :] // SB_TILE)).astype(BF16)
    masked_copies = pltpu.VMEM((2 * nb * n_pairs, lp, LANES), BF16)
    return pl.pallas_call(
        functools.partial(_sb_kernel, nb=nb),
        grid=(b // nb, lp // SB_TILE),
        in_specs=[
            pl.BlockSpec((nb, SB_TILE, D_SB), lambda i, q: (i, q, 0)),
            pl.BlockSpec((nb, lp, D_SB), lambda i, q: (i, 0, 1)),
            pl.BlockSpec((nb, lp, D_SB), lambda i, q: (i, 0, 2)),
            pl.BlockSpec((None, 1, D_SB), lambda i, q: (layer, 0, 0)),
            pl.BlockSpec((LANES, LANES), lambda i, q: (0, 0)),
            pl.BlockSpec(later2.shape, lambda i, q: (0, 0)),
        ],
        out_specs=pl.BlockSpec((nb, SB_TILE, D_SB), lambda i, q: (i, q, 0)),
        out_shape=jax.ShapeDtypeStruct((b, lp, D_SB), BF16),
        scratch_shapes=[masked_copies, masked_copies],
        compiler_params=_cparams(2),
        name="sb_group",
    )(ps, ps, ps, norm_g, bd128, later2)


def _block_diag_ones(n):
    i = jnp.arange(n) // HEAD_DIM
    return (i[:, None] == i[None, :]).astype(BF16)


@jax.jit
def _trunk(x, meta, ffn1_norm, ffn1_w13, ffn1_w2, mix_norm, w_in, conv_w, conv_b, conv_ln_g, conv_ln_b, rwkv_mu,
           rwkv_w0, rwkv_wB, rwkv_a0, rwkv_aB, rwkv_gB, rwkv_kk, rwkv_ka, rwkv_rk, rwkv_ln_g, rwkv_ln_b, sb_norm,
           w_out, ffn2_norm, ffn2_w13, ffn2_w2, final_norm):
    bsz, seq, d = x.shape
    depth = w_in.shape[0]
    l_real = N_META + seq
    lp = -(-l_real // SEQ_ALIGN) * SEQ_ALIGN
    meta_b = jnp.broadcast_to(meta.astype(x.dtype)[None], (bsz, N_META, d))
    h = jnp.concatenate([meta_b, x, jnp.zeros((bsz, lp - l_real, d), x.dtype)], axis=1).reshape(bsz * lp, d)

    row3 = lambda p: p.reshape(depth, 1, -1)
    ffn1_w13, ffn1_w2, ffn2_w13, ffn2_w2 = (w.astype(BF16) for w in (ffn1_w13, ffn1_w2, ffn2_w13, ffn2_w2))
    w_in, w_out = w_in.astype(BF16), w_out.astype(BF16)
    ffn1_norm, mix_norm, ffn2_norm, sb_norm = row3(ffn1_norm), row3(mix_norm), row3(ffn2_norm), row3(sb_norm)
    conv_b, conv_ln_g, conv_ln_b = row3(conv_b), row3(conv_ln_g), row3(conv_ln_b)
    rwkv_vecs = [row3(p) for p in (rwkv_mu, rwkv_w0, rwkv_a0, rwkv_kk, rwkv_ka, rwkv_rk, rwkv_ln_g, rwkv_ln_b)]
    mu, w0, a0, k_k, k_a, r_k, ln_g, ln_b = rwkv_vecs
    wb_ext = jnp.pad(rwkv_wB, ((0, 0), (0, AAA_LORA), (0, 0))).astype(BF16)
    ab_ext = jnp.pad(rwkv_aB, ((0, 0), (DECAY_LORA, 0), (0, 0))).astype(BF16)
    gb = rwkv_gB.astype(BF16)
    bd_rwkv = _block_diag_ones(D_RWKV)
    bd_pair = _block_diag_ones(LANES)

    for l in range(depth):
        h = _ffn(h, ffn1_norm, ffn1_w13, ffn1_w2, l, tm=1024)
        pc, pr, ps = _inproj(h, mix_norm, w_in, l)
        yr, yc = _rwkv_conv_group(pr.reshape(bsz, lp, -1), pc.reshape(bsz, lp, -1),
                                  (mu, w0, wb_ext, a0, ab_ext, gb, k_k, k_a, r_k, ln_g, ln_b),
                                  (conv_w, conv_b, conv_ln_g, conv_ln_b), bd_rwkv, l)
        ys = _sb_group(ps.reshape(bsz, lp, -1), sb_norm, bd_pair, l)
        mix = (yc.reshape(bsz * lp, -1), yr.reshape(bsz * lp, -1), ys.reshape(bsz * lp, -1), w_out)
        h = _ffn(h, ffn2_norm, ffn2_w13, ffn2_w2, l, mix=mix,
                 final_g=final_norm.reshape(1, d) if l == depth - 1 else None)
    out = h
    return out.reshape(bsz, lp, d)[:, N_META:l_real]


def kernel(x, meta, ffn1_norm, ffn1_w13, ffn1_w2, mix_norm, w_in, conv_w, conv_b, conv_ln_g, conv_ln_b, rwkv_mu,
           rwkv_w0, rwkv_wB, rwkv_a0, rwkv_aB, rwkv_gB, rwkv_kk, rwkv_ka, rwkv_rk, rwkv_ln_g, rwkv_ln_b, sb_norm,
           w_out, ffn2_norm, ffn2_w13, ffn2_w2, final_norm):
    return _trunk(x, meta, ffn1_norm, ffn1_w13, ffn1_w2, mix_norm, w_in, conv_w, conv_b, conv_ln_g, conv_ln_b,
                  rwkv_mu, rwkv_w0, rwkv_wB, rwkv_a0, rwkv_aB, rwkv_gB, rwkv_kk, rwkv_ka, rwkv_rk, rwkv_ln_g,
                  rwkv_ln_b, sb_norm, w_out, ffn2_norm, ffn2_w13, ffn2_w2, final_norm)
```

```python
import functools

import jax
import jax.numpy as jnp
from jax import lax
from jax.experimental import pallas as pl
from jax.experimental.pallas import tpu as pltpu

F32 = jnp.float32
BF16 = jnp.bfloat16

N_META = 16
HEAD_DIM = 64
D_CONV = 256
CONV_WIDTH = 31
D_RWKV = 384
D_SB = 384
DECAY_LORA = 64
AAA_LORA = 64
GATE_LORA = 128
D_RWKV_IN = 3 * D_RWKV + DECAY_LORA + AAA_LORA + GATE_LORA
RMS_EPS = 1e-6
LN_EPS = 1e-5
GN_EPS = 64e-5

LANES = 128
SEQ_ALIGN = 128
RWKV_CHUNK = 64
VMEM_LIMIT = 56 * 1024 * 1024


def _cparams(n_axes):
    return pltpu.CompilerParams(dimension_semantics=("arbitrary",) * n_axes, vmem_limit_bytes=VMEM_LIMIT)


def _dot(a, b):
    return jnp.dot(a.astype(BF16), b.astype(BF16), preferred_element_type=F32)


def _dot_nt(a, b):
    return lax.dot_general(a.astype(BF16), b.astype(BF16), (((1,), (1,)), ((), ())), preferred_element_type=F32)


def _split3(x):
    h1 = x.astype(BF16)
    r1 = x - h1.astype(F32)
    h2 = r1.astype(BF16)
    return h1, h2, (r1 - h2.astype(F32)).astype(BF16)


def _split2(x):
    hi = x.astype(BF16)
    return hi, (x - hi.astype(F32)).astype(BF16)


def _sigmoid(x):
    return 0.5 * jnp.tanh(0.5 * x) + 0.5


def _softplus(x):
    return jnp.maximum(x, 0.0) + jnp.log(1.0 + jnp.exp(-jnp.abs(x)))


def _rms_norm_rows(x, g):
    ms = jnp.mean(x * x, axis=-1, keepdims=True)
    return x * lax.rsqrt(ms + RMS_EPS) * g


def _ffn_kernel(h_ref, yc_ref, yr_ref, ys_ref, wo_ref, g_ref, w13_ref, w2_ref, fg_ref, o_ref, act_ref, *,
                d_ff, tf, mix, final):
    hin = h_ref[...]
    if mix:
        r0 = D_CONV
        r1 = D_CONV + D_RWKV
        hin = hin + jnp.dot(yc_ref[...], wo_ref[0:r0, :], preferred_element_type=F32)
        hin = hin + jnp.dot(yr_ref[...], wo_ref[r0:r1, :], preferred_element_type=F32)
        hin = hin + jnp.dot(ys_ref[...], wo_ref[r1:, :], preferred_element_type=F32)
    o_ref[...] = hin
    xn = _rms_norm_rows(hin, g_ref[...]).astype(BF16)
    for c0 in range(0, d_ff, tf):
        gate = jnp.dot(xn, w13_ref[:, c0:c0 + tf], preferred_element_type=F32)
        up = jnp.dot(xn, w13_ref[:, d_ff + c0:d_ff + c0 + tf], preferred_element_type=F32)
        act_ref[:, c0:c0 + tf] = (gate * _sigmoid(gate) * up).astype(BF16)
    out = o_ref[...] + 0.5 * jnp.dot(act_ref[...], w2_ref[...], preferred_element_type=F32)
    if final:
        out = _rms_norm_rows(out, fg_ref[...])
    o_ref[...] = out


def _ffn(h2d, norm_g, w13, w2, layer, *, mix=None, final_g=None, tm=512, tf=256):
    m, d = h2d.shape
    d_ff = w2.shape[1]
    assert m % tm == 0 and d_ff % tf == 0
    resident = dict(pipeline_mode=pl.Buffered(1))
    rows = lambda w: pl.BlockSpec((tm, w), lambda i: (i, 0))
    if mix is None:
        mix_args = [jnp.zeros((tm, w), BF16) for w in (D_CONV, D_RWKV, D_SB)] + [jnp.zeros((1, 16, d), BF16)]
        mix_specs = [pl.BlockSpec((tm, w), lambda i: (0, 0)) for w in (D_CONV, D_RWKV, D_SB)]
        wo_layer = 0
    else:
        mix_args = list(mix)
        mix_specs = [rows(D_CONV), rows(D_RWKV), rows(D_SB)]
        wo_layer = layer
    mix_specs.append(pl.BlockSpec((None,) + mix_args[3].shape[1:], lambda i: (wo_layer, 0, 0), **resident))
    fg = jnp.zeros((1, d), F32) if final_g is None else final_g
    return pl.pallas_call(
        functools.partial(_ffn_kernel, d_ff=d_ff, tf=tf, mix=mix is not None, final=final_g is not None),
        grid=(m // tm,),
        in_specs=[rows(d)] + mix_specs + [
            pl.BlockSpec((None, 1, d), lambda i: (layer, 0, 0)),
            pl.BlockSpec((None, d, 2 * d_ff), lambda i: (layer, 0, 0), **resident),
            pl.BlockSpec((None, d_ff, d), lambda i: (layer, 0, 0), **resident),
            pl.BlockSpec((1, d), lambda i: (0, 0)),
        ],
        out_specs=rows(d),
        out_shape=jax.ShapeDtypeStruct((m, d), F32),
        scratch_shapes=[pltpu.VMEM((tm, d_ff), BF16)],
        compiler_params=_cparams(1),
        name="ffn",
    )(h2d, *mix_args, norm_g, w13, w2, fg)


def _inproj_kernel(h_ref, g_ref, w_ref, pc_ref, pr_ref, ps_ref):
    xn = _rms_norm_rows(h_ref[...], g_ref[...]).astype(BF16)
    c0 = 2 * D_CONV
    c1 = c0 + D_RWKV_IN
    pc_ref[...] = jnp.dot(xn, w_ref[:, 0:c0], preferred_element_type=F32)
    pr_ref[...] = jnp.dot(xn, w_ref[:, c0:c1], preferred_element_type=F32)
    ps_ref[...] = jnp.dot(xn, w_ref[:, c1:], preferred_element_type=F32).astype(BF16)


def _inproj(h2d, norm_g, w_in, layer, *, tm=512):
    m, d = h2d.shape
    assert m % tm == 0
    d_in = w_in.shape[2]
    widths = (2 * D_CONV, D_RWKV_IN, 3 * D_SB)
    return pl.pallas_call(
        _inproj_kernel,
        grid=(m // tm,),
        in_specs=[
            pl.BlockSpec((tm, d), lambda i: (i, 0)),
            pl.BlockSpec((None, 1, d), lambda i: (layer, 0, 0)),
            pl.BlockSpec((None, d, d_in), lambda i: (layer, 0, 0)),
        ],
        out_specs=[pl.BlockSpec((tm, w), lambda i: (i, 0)) for w in widths],
        out_shape=[jax.ShapeDtypeStruct((m, w), dt) for w, dt in zip(widths, (F32, F32, BF16))],
        compiler_params=_cparams(1),
        name="inproj",
    )(h2d, norm_g, w_in)


CONV_PAD = 32
CONV_SUB = 64


def _conv_tile(e, p_ref, w_ref, b_ref, lg_ref, lb_ref, o_ref, buf_ref, shift_ref, tt):
    val = p_ref[e, :, 0:D_CONV]
    gate = p_ref[e, :, D_CONV:2 * D_CONV]
    buf_ref[e, CONV_PAD:CONV_PAD + tt, :] = val * _sigmoid(gate)
    first = CONV_PAD - (CONV_WIDTH - 1)
    for b in range(8):
        span = tt + 8 * ((CONV_WIDTH - 1 - b) // 8)
        shift_ref[e, b, 0:span, :] = buf_ref[e, first + b:first + b + span, :]
    for s in range(tt // CONV_SUB):
        acc = jnp.zeros((CONV_SUB, D_CONV), F32)
        for j in range(CONV_WIDTH):
            r0 = s * CONV_SUB + 8 * (j // 8)
            acc = acc + w_ref[j:j + 1, :] * shift_ref[e, j % 8, r0:r0 + CONV_SUB, :]
        y = acc + b_ref[...]
        mu = jnp.mean(y, axis=-1, keepdims=True)
        dlt = y - mu
        var = jnp.mean(dlt * dlt, axis=-1, keepdims=True)
        yn = dlt * lax.rsqrt(var + LN_EPS) * lg_ref[...] + lb_ref[...]
        o_ref[e, s * CONV_SUB:(s + 1) * CONV_SUB, :] = (yn * _sigmoid(yn)).astype(BF16)
    buf_ref[e, 0:CONV_PAD, :] = buf_ref[e, tt:tt + CONV_PAD, :]


def _rwkv_conv_kernel(p_ref, mu_ref, w0_ref, wb_ref, a0_ref, ab_ref, gb_ref, kk_ref, ka_ref, rk_ref, lg_ref, lb_ref,
                      bd_ref, tri_ref, pc_ref, cw_ref, cb_ref, clg_ref, clb_ref, o_ref, oc_ref,
                      st_ref, carry_ref, buf_ref, shift_ref, *, nb, tt):
    c = RWKV_CHUNK
    c2 = 2 * c
    n_pairs = D_RWKV // LANES
    chunks_per_seq = tt // c
    n_chunks = nb * chunks_per_seq
    t = pl.program_id(1)

    @pl.when(t == 0)
    def _():
        st_ref[...] = jnp.zeros_like(st_ref)
        carry_ref[...] = jnp.zeros_like(carry_ref)
        buf_ref[:, 0:CONV_PAD, :] = jnp.zeros((nb, CONV_PAD, D_CONV), F32)


    x = p_ref[...].reshape(nb * tt, D_RWKV_IN)
    row = lax.broadcasted_iota(jnp.int32, x.shape, 0)
    prev = pltpu.roll(x, 1, 0)
    for e in range(nb):
        prev = jnp.where(row == e * tt, carry_ref[e], prev)
        carry_ref[e] = x[(e + 1) * tt - 1:(e + 1) * tt, :]
    xs = x + mu_ref[...] * (prev - x)

    r = xs[:, 0:D_RWKV]
    k = xs[:, D_RWKV:2 * D_RWKV]
    v = xs[:, 2 * D_RWKV:3 * D_RWKV]
    lora_in = xs[:, 3 * D_RWKV:3 * D_RWKV + DECAY_LORA + AAA_LORA]
    gd = xs[:, 3 * D_RWKV + DECAY_LORA + AAA_LORA:]

    bd = bd_ref[...]

    def head_sum(z):
        hi, lo = _split2(z)
        return jnp.dot(hi, bd, preferred_element_type=F32) + jnp.dot(lo, bd, preferred_element_type=F32)

    w_log = -_softplus(-(w0_ref[...] + _dot(jnp.tanh(lora_in), wb_ref[...]))) - 0.5
    logw = -jnp.exp(w_log)
    a = _sigmoid(a0_ref[...] + _dot(lora_in, ab_ref[...]))
    g = _dot(_sigmoid(gd), gb_ref[...])
    kk = k * kk_ref[...]
    kk = kk * lax.rsqrt(jnp.maximum(head_sum(kk * kk), 1e-24))
    kq = k * (1.0 + (a - 1.0) * ka_ref[...])
    na = -kk
    b = kk * a
    bonus = head_sum(r * kq * rk_ref[...]) * v

    tri = tri_ref[...]
    l1, l2, l3 = _split3(logw)
    cl = (jnp.dot(tri, l1, preferred_element_type=F32) + jnp.dot(tri, l2, preferred_element_type=F32)
          + jnp.dot(tri, l3, preferred_element_type=F32))
    cl_ends = [cl[(s + 1) * c - 1:(s + 1) * c, :] for s in range(n_chunks)]
    cl_end = jnp.concatenate([jnp.broadcast_to(e, (c, D_RWKV)) for e in cl_ends], axis=0)
    e_pos = jnp.exp(cl)
    e_neg = jnp.exp(-cl)
    e_end = jnp.exp(cl_end - cl)
    at_all = jnp.exp(cl - logw) * na
    bt_all = b * e_neg
    kt_all = kq * e_neg
    rt_all = r * e_pos
    bp_all = b * e_end
    kp_all = kq * e_end

    lane = lax.broadcasted_iota(jnp.int32, (c, LANES), 1)
    lane2 = lax.broadcasted_iota(jnp.int32, (c2, LANES), 1)
    m0 = lane < HEAD_DIM
    m0_2 = lane2 < HEAD_DIM
    ri = lax.broadcasted_iota(jnp.int32, (c2, c2), 0)
    cj = lax.broadcasted_iota(jnp.int32, (c2, c2), 1)
    ti = jnp.bitwise_and(ri, c - 1)
    sj = jnp.bitwise_and(cj, c - 1)
    mask_z = (ti > sj) | ((ri >= c) & (ti == sj))
    left_top = lax.broadcasted_iota(jnp.int32, (c, c2), 1) < c
    blockdiag = jnp.bitwise_and(ri, c) == jnp.bitwise_and(cj, c)
    eye = (ri == cj)
    zero = jnp.zeros((c, LANES), F32)
    rows = lambda parts: jnp.concatenate(parts, axis=0)
    cols = lambda parts: jnp.concatenate(parts, axis=1)

    units = [(s, j) for s in range(n_chunks) for j in range(n_pairs)]

    def tile(arr, u):
        s, j = u
        return arr[s * c:(s + 1) * c, j * LANES:(j + 1) * LANES]

    at = [tile(at_all, u) for u in units]
    rt = [tile(rt_all, u) for u in units]
    vv = [tile(v, u) for u in units]

    z0, z1 = [], []
    for i, u in enumerate(units):
        xx = rows([at[i], rt[i]])
        bt, kt = tile(bt_all, u), tile(kt_all, u)
        z0.append(jnp.where(mask_z, _dot_nt(jnp.where(m0_2, xx, 0.0), rows([bt, kt])), 0.0))
        z1.append(jnp.where(mask_z, _dot_nt(jnp.where(m0_2, 0.0, xx), rows([kt, bt])), 0.0))

    pw = [rows([jnp.where(left_top, z0[i][0:c], 0.0), jnp.where(left_top, 0.0, z1[i][0:c])]) for i in range(len(units))]
    identity = jnp.where(eye, 1.0, 0.0)
    tinv = [identity + p for p in pw]
    for _ in range(5):
        pw = [_dot(p, p) for p in pw]
        tinv = [ti_ + _dot(ti_, p) for ti_, p in zip(tinv, pw)]

    akv = []
    for i in range(len(units)):
        v2 = rows([vv[i], vv[i]])
        akv.append(jnp.where(m0, _dot(jnp.where(left_top, 0.0, z0[i][0:c]), v2),
                             _dot(jnp.where(left_top, z1[i][0:c], 0.0), v2)))

    ah, uv = [], []
    for i in range(len(units)):
        wa = rows([jnp.where(m0, at[i], 0.0), jnp.where(m0, 0.0, at[i])])
        wv = rows([jnp.where(m0, akv[i], 0.0), jnp.where(m0, 0.0, akv[i])])
        tw = _dot(tinv[i], cols([wa, wv]))
        ah.append(tw[0:c, 0:LANES] + tw[c:c2, 0:LANES])
        uv.append(tw[0:c, LANES:] + tw[c:c2, LANES:])

    rh, ov, gmat, hmat = [], [], [], []
    for i, u in enumerate(units):
        rhs0 = cols([rows([ah[i], zero]), rows([uv[i], vv[i]])])
        rhs1 = cols([rows([zero, ah[i]]), rows([vv[i], uv[i]])])
        res0 = _dot(z0[i][c:c2], rhs0)
        res1 = _dot(z1[i][c:c2], rhs1)
        rh.append(rt[i] + jnp.where(m0, res0[:, 0:LANES], res1[:, 0:LANES]))
        ov.append(jnp.where(m0, res0[:, LANES:], res1[:, LANES:]))
        ypt = rows([tile(bp_all, u), tile(kp_all, u)]).T
        gh = _dot(ypt, rhs0)
        s, j = u
        p_end = jnp.exp(cl_ends[s][:, j * LANES:(j + 1) * LANES])
        gmat.append(jnp.where(blockdiag, gh[:, 0:LANES], 0.0)
                    + jnp.where(eye, jnp.broadcast_to(p_end, (c2, LANES)), 0.0))
        hmat.append(jnp.where(blockdiag, gh[:, LANES:], 0.0))

    states = [st_ref[n] for n in range(nb * n_pairs)]
    out_rows = []
    for s in range(n_chunks):
        out_pairs = []
        for j in range(n_pairs):
            i = s * n_pairs + j
            n = (s // chunks_per_seq) * n_pairs + j
            out_pairs.append(_dot(rh[i], states[n]) + ov[i])
            states[n] = _dot(gmat[i], states[n]) + hmat[i]
        out_rows.append(cols(out_pairs))
    for n in range(nb * n_pairs):
        st_ref[n] = states[n]

    o = rows(out_rows)
    mean = head_sum(o) * (1.0 / HEAD_DIM)
    dlt = o - mean
    var = head_sum(dlt * dlt) * (1.0 / HEAD_DIM)
    on = dlt * lax.rsqrt(var + GN_EPS) * lg_ref[...] + lb_ref[...]
    o_ref[...] = ((on + bonus) * g).astype(BF16).reshape(nb, tt, D_RWKV)

    for e in range(nb):
        _conv_tile(e, pc_ref, cw_ref, cb_ref, clg_ref, clb_ref, oc_ref, buf_ref, shift_ref, tt)


def _rwkv_conv_group(pr, pc, rwkv_params, conv_params, bd, layer, *, nb=4, tt=64):
    b, lp, _ = pr.shape
    assert b % nb == 0 and lp % tt == 0 and tt % RWKV_CHUNK == 0 and tt % CONV_SUB == 0
    n_pairs = D_RWKV // LANES
    idx = jnp.arange(nb * tt)
    tri = ((idx[:, None] >= idx[None, :]) & (idx[:, None] // RWKV_CHUNK == idx[None, :] // RWKV_CHUNK)).astype(BF16)

    def par(arr):
        return pl.BlockSpec((None,) + arr.shape[1:], lambda i, t: (layer,) + (0,) * (arr.ndim - 1))

    def const(arr):
        return pl.BlockSpec(arr.shape, lambda i, t: (0,) * arr.ndim)

    tile = lambda w: pl.BlockSpec((nb, tt, w), lambda i, t: (i, t, 0))
    return pl.pallas_call(
        functools.partial(_rwkv_conv_kernel, nb=nb, tt=tt),
        grid=(b // nb, lp // tt),
        in_specs=[tile(D_RWKV_IN)] + [par(p) for p in rwkv_params] + [const(bd), const(tri)]
        + [tile(2 * D_CONV)] + [par(p) for p in conv_params],
        out_specs=[tile(D_RWKV), tile(D_CONV)],
        out_shape=[jax.ShapeDtypeStruct((b, lp, D_RWKV), BF16), jax.ShapeDtypeStruct((b, lp, D_CONV), BF16)],
        scratch_shapes=[pltpu.VMEM((nb * n_pairs, LANES, LANES), F32), pltpu.VMEM((nb, 1, D_RWKV_IN), F32),
                        pltpu.VMEM((nb, tt + CONV_PAD, D_CONV), F32),
                        pltpu.VMEM((nb, 8, tt + 8 * ((CONV_WIDTH - 1) // 8), D_CONV), F32)],
        compiler_params=_cparams(2),
        name="rwkv_conv_group",
    )(pr, *rwkv_params, bd, tri, pc, *conv_params)


SB_TILE = 128
SB_GROUP_TILES = 8


def _sb_kernel(q_ref, k_ref, v_ref, g_ref, bd_ref, later_ref, o_ref, km_ref, vm_ref, *, nb):
    t = SB_TILE
    n_pairs = D_SB // LANES
    n_streams = nb * n_pairs
    qi = pl.program_id(1)

    @pl.when(qi == 0)
    def _():
        m0 = lax.broadcasted_iota(jnp.int32, (k_ref.shape[1], LANES), 1) < HEAD_DIM
        for p in range(n_streams):
            e, j = divmod(p, n_pairs)
            kj = k_ref[e, :, j * LANES:(j + 1) * LANES]
            vj = v_ref[e, :, j * LANES:(j + 1) * LANES]
            zero = jnp.zeros_like(kj)
            km_ref[2 * p] = jnp.where(m0, kj, zero)
            km_ref[2 * p + 1] = jnp.where(m0, zero, kj)
            vm_ref[2 * p] = jnp.where(m0, vj, zero)
            vm_ref[2 * p + 1] = jnp.where(m0, zero, vj)

    rowi = lax.broadcasted_iota(jnp.int32, (t, t), 0)
    coli = lax.broadcasted_iota(jnp.int32, (t, t), 1)
    causal = coli < rowi
    causal2 = jnp.concatenate([causal, causal], axis=1)
    later2 = later_ref[...]
    qb = [q_ref[p // n_pairs, :, (p % n_pairs) * LANES:(p % n_pairs + 1) * LANES] * (HEAD_DIM ** -0.5)
          for p in range(n_streams)]

    def both_heads(ref, kb, p):
        off = kb * t if isinstance(kb, int) else pl.multiple_of(kb * t, t)
        return jnp.concatenate([ref[2 * p, pl.ds(off, t), :], ref[2 * p + 1, pl.ds(off, t), :]], axis=0)

    def scores(kbs):
        return tuple(lax.dot_general(qb[p], both_heads(km_ref, kb, p), (((1,), (1,)), ((), ())),
                                     preferred_element_type=F32) for kb in kbs for p in range(n_streams))

    def group(kbs, zs, accs, carries, diagonal):
        units = [(kb, p) for kb in kbs for p in range(n_streams)]
        sps = [_softplus(z.astype(BF16)) for z in zs]
        if diagonal:
            sps = [jnp.where(causal2, sp, jnp.zeros_like(sp)) for sp in sps]
        locs = [jnp.dot(sp, later2, preferred_element_type=F32) for sp in sps]
        accs, carries = list(accs), list(carries)
        for i, (kb, p) in enumerate(units):
            halves = []
            for m in range(2):
                sl = slice(m * t, (m + 1) * t)
                att = jnp.exp(zs[i][:, sl] - locs[i][:, sl] - carries[2 * p + m])
                if diagonal:
                    att = jnp.where(causal, att, 0.0)
                halves.append(att.astype(BF16))
                carries[2 * p + m] = carries[2 * p + m] + locs[i][:, m * t:m * t + 1]
            accs[p] = accs[p] + jnp.dot(jnp.concatenate(halves, axis=1), both_heads(vm_ref, kb, p),
                                        preferred_element_type=F32)
        return tuple(accs), tuple(carries)

    state = (tuple(jnp.zeros((t, LANES), F32) for _ in range(n_streams)),
             tuple(jnp.zeros((t, 1), F32) for _ in range(2 * n_streams)))
    def head_blocks(r):
        def f(state):
            state = group([qi], scores([qi]), *state, True)
            for w in range(r):
                state = group([qi - 1 - w], scores([qi - 1 - w]), *state, False)
            return state
        return f

    width = max(1, SB_GROUP_TILES // nb)
    n_head = jnp.bitwise_and(qi, width - 1)
    state = lax.switch(n_head, [head_blocks(r) for r in range(width)], state)

    def run(first, n_iters, width, state):
        def body(it, c):
            kbs = [first - width * it - w for w in range(width)]
            zs_next = scores(kbs[0:1])
            for w in range(width):
                zs = zs_next
                if w + 1 < width:
                    zs_next = scores(kbs[w + 1:w + 2])
                c = group(kbs[w:w + 1], zs, *c, False)
            return c
        return lax.fori_loop(0, n_iters, body, state)

    state = run(qi - 1 - n_head, jnp.right_shift(qi, width.bit_length() - 1), width, state)
    accs = state[0]

    for p in range(n_streams):
        e, j = divmod(p, n_pairs)
        o = accs[p]
        hi, lo = _split2(o * o)
        ss = jnp.dot(hi, bd_ref[...], preferred_element_type=F32) + jnp.dot(lo, bd_ref[...], preferred_element_type=F32)
        o_ref[e, :, j * LANES:(j + 1) * LANES] = (o * lax.rsqrt(ss * (1.0 / HEAD_DIM) + RMS_EPS)
                                                  * g_ref[:, j * LANES:(j + 1) * LANES]).astype(BF16)


def _sb_group(ps, norm_g, bd128, layer, *, nb=2):
    b, lp, _ = ps.shape
    assert lp % SB_TILE == 0 and b % nb == 0
    n_pairs = D_SB // LANES
    idx = jnp.arange(2 * SB_TILE)
    later2 = ((idx[:, None] >= idx[None, :]) & (idx[:, None] // SB_TILE == idx[None, :] // SB_TILE)).astype(BF16)
    masked_copies = pltpu.VMEM((2 * nb * n_pairs, lp, LANES), BF16)
    return pl.pallas_call(
        functools.partial(_sb_kernel, nb=nb),
        grid=(b // nb, lp // SB_TILE),
        in_specs=[
            pl.BlockSpec((nb, SB_TILE, D_SB), lambda i, q: (i, q, 0)),
            pl.BlockSpec((nb, lp, D_SB), lambda i, q: (i, 0, 1)),
            pl.BlockSpec((nb, lp, D_SB), lambda i, q: (i, 0, 2)),
            pl.BlockSpec((None, 1, D_SB), lambda i, q: (layer, 0, 0)),
            pl.BlockSpec((LANES, LANES), lambda i, q: (0, 0)),
            pl.BlockSpec(later2.shape, lambda i, q: (0, 0)),
        ],
        out_specs=pl.BlockSpec((nb, SB_TILE, D_SB), lambda i, q: (i, q, 0)),
        out_shape=jax.ShapeDtypeStruct((b, lp, D_SB), BF16),
        scratch_shapes=[masked_copies, masked_copies],
        compiler_params=_cparams(2),
        name="sb_group",
    )(ps, ps, ps, norm_g, bd128, later2)


def _block_diag_ones(n):
    i = jnp.arange(n) // HEAD_DIM
    return (i[:, None] == i[None, :]).astype(BF16)


@jax.jit
def _trunk(x, meta, ffn1_norm, ffn1_w13, ffn1_w2, mix_norm, w_in, conv_w, conv_b, conv_ln_g, conv_ln_b, rwkv_mu,
           rwkv_w0, rwkv_wB, rwkv_a0, rwkv_aB, rwkv_gB, rwkv_kk, rwkv_ka, rwkv_rk, rwkv_ln_g, rwkv_ln_b, sb_norm,
           w_out, ffn2_norm, ffn2_w13, ffn2_w2, final_norm):
    bsz, seq, d = x.shape
    depth = w_in.shape[0]
    l_real = N_META + seq
    lp = -(-l_real // SEQ_ALIGN) * SEQ_ALIGN
    meta_b = jnp.broadcast_to(meta.astype(x.dtype)[None], (bsz, N_META, d))
    h = jnp.concatenate([meta_b, x, jnp.zeros((bsz, lp - l_real, d), x.dtype)], axis=1).reshape(bsz * lp, d)

    row3 = lambda p: p.reshape(depth, 1, -1)
    ffn1_w13, ffn1_w2, ffn2_w13, ffn2_w2 = (w.astype(BF16) for w in (ffn1_w13, ffn1_w2, ffn2_w13, ffn2_w2))
    w_in, w_out = w_in.astype(BF16), w_out.astype(BF16)
    ffn1_norm, mix_norm, ffn2_norm, sb_norm = row3(ffn1_norm), row3(mix_norm), row3(ffn2_norm), row3(sb_norm)
    conv_b, conv_ln_g, conv_ln_b = row3(conv_b), row3(conv_ln_g), row3(conv_ln_b)
    rwkv_vecs = [row3(p) for p in (rwkv_mu, rwkv_w0, rwkv_a0, rwkv_kk, rwkv_ka, rwkv_rk, rwkv_ln_g, rwkv_ln_b)]
    mu, w0, a0, k_k, k_a, r_k, ln_g, ln_b = rwkv_vecs
    wb_ext = jnp.pad(rwkv_wB, ((0, 0), (0, AAA_LORA), (0, 0))).astype(BF16)
    ab_ext = jnp.pad(rwkv_aB, ((0, 0), (DECAY_LORA, 0), (0, 0))).astype(BF16)
    gb = rwkv_gB.astype(BF16)
    bd_rwkv = _block_diag_ones(D_RWKV)
    bd_pair = _block_diag_ones(LANES)

    for l in range(depth):
        h = _ffn(h, ffn1_norm, ffn1_w13, ffn1_w2, l, tm=1024)
        pc, pr, ps = _inproj(h, mix_norm, w_in, l)
        yr, yc = _rwkv_conv_group(pr.reshape(bsz, lp, -1), pc.reshape(bsz, lp, -1),
                                  (mu, w0, wb_ext, a0, ab_ext, gb, k_k, k_a, r_k, ln_g, ln_b),
                                  (conv_w, conv_b, conv_ln_g, conv_ln_b), bd_rwkv, l)
        ys = _sb_group(ps.reshape(bsz, lp, -1), sb_norm, bd_pair, l)
        mix = (yc.reshape(bsz * lp, -1), yr.reshape(bsz * lp, -1), ys.reshape(bsz * lp, -1), w_out)
        h = _ffn(h, ffn2_norm, ffn2_w13, ffn2_w2, l, mix=mix,
                 final_g=final_norm.reshape(1, d) if l == depth - 1 else None)
    out = h
    return out.reshape(bsz, lp, d)[:, N_META:l_real]


def kernel(x, meta, ffn1_norm, ffn1_w13, ffn1_w2, mix_norm, w_in, conv_w, conv_b, conv_ln_g, conv_ln_b, rwkv_mu,
           rwkv_w0, rwkv_wB, rwkv_a0, rwkv_aB, rwkv_gB, rwkv_kk, rwkv_ka, rwkv_rk, rwkv_ln_g, rwkv_ln_b, sb_norm,
           w_out, ffn2_norm, ffn2_w13, ffn2_w2, final_norm):
    return _trunk(x, meta, ffn1_norm, ffn1_w13, ffn1_w2, mix_norm, w_in, conv_w, conv_b, conv_ln_g, conv_ln_b,
                  rwkv_mu, rwkv_w0, rwkv_wB, rwkv_a0, rwkv_aB, rwkv_gB, rwkv_kk, rwkv_ka, rwkv_rk, rwkv_ln_g,
                  rwkv_ln_b, sb_norm, w_out, ffn2_norm, ffn2_w13, ffn2_w2, final_norm)
```

```python
import functools

import jax
import jax.numpy as jnp
from jax import lax
from jax.experimental import pallas as pl
from jax.experimental.pallas import tpu as pltpu

F32 = jnp.float32
BF16 = jnp.bfloat16

N_META = 16
HEAD_DIM = 64
D_CONV = 256
CONV_WIDTH = 31
D_RWKV = 384
D_SB = 384
DECAY_LORA = 64
AAA_LORA = 64
GATE_LORA = 128
D_RWKV_IN = 3 * D_RWKV + DECAY_LORA + AAA_LORA + GATE_LORA
RMS_EPS = 1e-6
LN_EPS = 1e-5
GN_EPS = 64e-5

LANES = 128
SEQ_ALIGN = 128
RWKV_CHUNK = 64
VMEM_LIMIT = 56 * 1024 * 1024


def _cparams(n_axes):
    return pltpu.CompilerParams(dimension_semantics=("arbitrary",) * n_axes, vmem_limit_bytes=VMEM_LIMIT)


def _dot(a, b):
    return jnp.dot(a.astype(BF16), b.astype(BF16), preferred_element_type=F32)


def _dot_nt(a, b):
    return lax.dot_general(a.astype(BF16), b.astype(BF16), (((1,), (1,)), ((), ())), preferred_element_type=F32)


def _split3(x):
    h1 = x.astype(BF16)
    r1 = x - h1.astype(F32)
    h2 = r1.astype(BF16)
    return h1, h2, (r1 - h2.astype(F32)).astype(BF16)


def _split2(x):
    hi = x.astype(BF16)
    return hi, (x - hi.astype(F32)).astype(BF16)


def _sigmoid(x):
    return 0.5 * jnp.tanh(0.5 * x) + 0.5


def _softplus(x):
    return jnp.maximum(x, 0.0) + jnp.log(1.0 + jnp.exp(-jnp.abs(x)))


def _rms_norm_rows(x, g):
    ms = jnp.mean(x * x, axis=-1, keepdims=True)
    return x * lax.rsqrt(ms + RMS_EPS) * g


def _ffn_kernel(h_ref, yc_ref, yr_ref, ys_ref, wo_ref, g_ref, w13_ref, w2_ref, fg_ref, o_ref, act_ref, *,
                d_ff, tf, mix, final):
    hin = h_ref[...]
    if mix:
        r0 = D_CONV
        r1 = D_CONV + D_RWKV
        hin = hin + jnp.dot(yc_ref[...], wo_ref[0:r0, :], preferred_element_type=F32)
        hin = hin + jnp.dot(yr_ref[...], wo_ref[r0:r1, :], preferred_element_type=F32)
        hin = hin + jnp.dot(ys_ref[...], wo_ref[r1:, :], preferred_element_type=F32)
    o_ref[...] = hin
    xn = _rms_norm_rows(hin, g_ref[...]).astype(BF16)
    for c0 in range(0, d_ff, tf):
        gate = jnp.dot(xn, w13_ref[:, c0:c0 + tf], preferred_element_type=F32)
        up = jnp.dot(xn, w13_ref[:, d_ff + c0:d_ff + c0 + tf], preferred_element_type=F32)
        act_ref[:, c0:c0 + tf] = (gate * _sigmoid(gate) * up).astype(BF16)
    out = o_ref[...] + 0.5 * jnp.dot(act_ref[...], w2_ref[...], preferred_element_type=F32)
    if final:
        out = _rms_norm_rows(out, fg_ref[...])
    o_ref[...] = out


def _ffn(h2d, norm_g, w13, w2, layer, *, mix=None, final_g=None, tm=1024, tf=256):
    m, d = h2d.shape
    d_ff = w2.shape[1]
    assert m % tm == 0 and d_ff % tf == 0
    resident = dict(pipeline_mode=pl.Buffered(1))
    rows = lambda w: pl.BlockSpec((tm, w), lambda i: (i, 0))
    if mix is None:
        mix_args = [jnp.zeros((tm, w), BF16) for w in (D_CONV, D_RWKV, D_SB)] + [jnp.zeros((1, 16, d), BF16)]
        mix_specs = [pl.BlockSpec((tm, w), lambda i: (0, 0)) for w in (D_CONV, D_RWKV, D_SB)]
        wo_layer = 0
    else:
        mix_args = list(mix)
        mix_specs = [rows(D_CONV), rows(D_RWKV), rows(D_SB)]
        wo_layer = layer
    mix_specs.append(pl.BlockSpec((None,) + mix_args[3].shape[1:], lambda i: (wo_layer, 0, 0), **resident))
    fg = jnp.zeros((1, d), F32) if final_g is None else final_g
    return pl.pallas_call(
        functools.partial(_ffn_kernel, d_ff=d_ff, tf=tf, mix=mix is not None, final=final_g is not None),
        grid=(m // tm,),
        in_specs=[rows(d)] + mix_specs + [
            pl.BlockSpec((None, 1, d), lambda i: (layer, 0, 0)),
            pl.BlockSpec((None, d, 2 * d_ff), lambda i: (layer, 0, 0), **resident),
            pl.BlockSpec((None, d_ff, d), lambda i: (layer, 0, 0), **resident),
            pl.BlockSpec((1, d), lambda i: (0, 0)),
        ],
        out_specs=rows(d),
        out_shape=jax.ShapeDtypeStruct((m, d), F32),
        scratch_shapes=[pltpu.VMEM((tm, d_ff), BF16)],
        compiler_params=_cparams(1),
        name="ffn",
    )(h2d, *mix_args, norm_g, w13, w2, fg)


def _inproj_kernel(h_ref, g_ref, w_ref, pc_ref, pr_ref, ps_ref):
    xn = _rms_norm_rows(h_ref[...], g_ref[...]).astype(BF16)
    c0 = 2 * D_CONV
    c1 = c0 + D_RWKV_IN
    pc_ref[...] = jnp.dot(xn, w_ref[:, 0:c0], preferred_element_type=F32)
    pr_ref[...] = jnp.dot(xn, w_ref[:, c0:c1], preferred_element_type=F32)
    ps_ref[...] = jnp.dot(xn, w_ref[:, c1:], preferred_element_type=F32).astype(BF16)


def _inproj(h2d, norm_g, w_in, layer, *, tm=1024):
    m, d = h2d.shape
    assert m % tm == 0
    d_in = w_in.shape[2]
    widths = (2 * D_CONV, D_RWKV_IN, 3 * D_SB)
    return pl.pallas_call(
        _inproj_kernel,
        grid=(m // tm,),
        in_specs=[
            pl.BlockSpec((tm, d), lambda i: (i, 0)),
            pl.BlockSpec((None, 1, d), lambda i: (layer, 0, 0)),
            pl.BlockSpec((None, d, d_in), lambda i: (layer, 0, 0), pipeline_mode=pl.Buffered(1)),
        ],
        out_specs=[pl.BlockSpec((tm, w), lambda i: (i, 0)) for w in widths],
        out_shape=[jax.ShapeDtypeStruct((m, w), dt) for w, dt in zip(widths, (F32, F32, BF16))],
        compiler_params=_cparams(1),
        name="inproj",
    )(h2d, norm_g, w_in)


CONV_PAD = 32
CONV_SUB = 64


def _conv_tile(e, p_ref, w_ref, b_ref, lg_ref, lb_ref, o_ref, buf_ref, shift_ref, tt):
    val = p_ref[e, :, 0:D_CONV]
    gate = p_ref[e, :, D_CONV:2 * D_CONV]
    buf_ref[e, CONV_PAD:CONV_PAD + tt, :] = val * _sigmoid(gate)
    first = CONV_PAD - (CONV_WIDTH - 1)
    for b in range(8):
        span = tt + 8 * ((CONV_WIDTH - 1 - b) // 8)
        shift_ref[e, b, 0:span, :] = buf_ref[e, first + b:first + b + span, :]
    for s in range(tt // CONV_SUB):
        acc = jnp.zeros((CONV_SUB, D_CONV), F32)
        for j in range(CONV_WIDTH):
            r0 = s * CONV_SUB + 8 * (j // 8)
            acc = acc + w_ref[j:j + 1, :] * shift_ref[e, j % 8, r0:r0 + CONV_SUB, :]
        y = acc + b_ref[...]
        mu = jnp.mean(y, axis=-1, keepdims=True)
        dlt = y - mu
        var = jnp.mean(dlt * dlt, axis=-1, keepdims=True)
        yn = dlt * lax.rsqrt(var + LN_EPS) * lg_ref[...] + lb_ref[...]
        o_ref[e, s * CONV_SUB:(s + 1) * CONV_SUB, :] = (yn * _sigmoid(yn)).astype(BF16)
    buf_ref[e, 0:CONV_PAD, :] = buf_ref[e, tt:tt + CONV_PAD, :]


def _rwkv_conv_kernel(p_ref, mu_ref, w0_ref, wb_ref, a0_ref, ab_ref, gb_ref, kk_ref, ka_ref, rk_ref, lg_ref, lb_ref,
                      bd_ref, tri_ref, pc_ref, cw_ref, cb_ref, clg_ref, clb_ref, o_ref, oc_ref,
                      st_ref, carry_ref, buf_ref, shift_ref, *, nb, tt):
    c = RWKV_CHUNK
    c2 = 2 * c
    n_pairs = D_RWKV // LANES
    chunks_per_seq = tt // c
    n_chunks = nb * chunks_per_seq
    t = pl.program_id(1)

    @pl.when(t == 0)
    def _():
        st_ref[...] = jnp.zeros_like(st_ref)
        carry_ref[...] = jnp.zeros_like(carry_ref)
        buf_ref[:, 0:CONV_PAD, :] = jnp.zeros((nb, CONV_PAD, D_CONV), F32)


    x = p_ref[...].reshape(nb * tt, D_RWKV_IN)
    row = lax.broadcasted_iota(jnp.int32, x.shape, 0)
    prev = pltpu.roll(x, 1, 0)
    for e in range(nb):
        prev = jnp.where(row == e * tt, carry_ref[e], prev)
        carry_ref[e] = x[(e + 1) * tt - 1:(e + 1) * tt, :]
    xs = x + mu_ref[...] * (prev - x)

    r = xs[:, 0:D_RWKV]
    k = xs[:, D_RWKV:2 * D_RWKV]
    v = xs[:, 2 * D_RWKV:3 * D_RWKV]
    lora_in = xs[:, 3 * D_RWKV:3 * D_RWKV + DECAY_LORA + AAA_LORA]
    gd = xs[:, 3 * D_RWKV + DECAY_LORA + AAA_LORA:]

    bd = bd_ref[...]

    def head_sum(z):
        hi, lo = _split2(z)
        return jnp.dot(hi, bd, preferred_element_type=F32) + jnp.dot(lo, bd, preferred_element_type=F32)

    w_log = -_softplus(-(w0_ref[...] + _dot(jnp.tanh(lora_in), wb_ref[...]))) - 0.5
    logw = -jnp.exp(w_log)
    a = _sigmoid(a0_ref[...] + _dot(lora_in, ab_ref[...]))
    g = _dot(_sigmoid(gd), gb_ref[...])
    kk = k * kk_ref[...]
    kk = kk * lax.rsqrt(jnp.maximum(head_sum(kk * kk), 1e-24))
    kq = k * (1.0 + (a - 1.0) * ka_ref[...])
    na = -kk
    b = kk * a
    bonus = head_sum(r * kq * rk_ref[...]) * v

    tri = tri_ref[...]
    l1, l2, l3 = _split3(logw)
    cl = (jnp.dot(tri, l1, preferred_element_type=F32) + jnp.dot(tri, l2, preferred_element_type=F32)
          + jnp.dot(tri, l3, preferred_element_type=F32))
    cl_ends = [cl[(s + 1) * c - 1:(s + 1) * c, :] for s in range(n_chunks)]
    cl_end = jnp.concatenate([jnp.broadcast_to(e, (c, D_RWKV)) for e in cl_ends], axis=0)
    e_pos = jnp.exp(cl)
    e_neg = jnp.exp(-cl)
    e_end = jnp.exp(cl_end - cl)
    at_all = jnp.exp(cl - logw) * na
    bt_all = b * e_neg
    kt_all = kq * e_neg
    rt_all = r * e_pos
    bp_all = b * e_end
    kp_all = kq * e_end

    lane = lax.broadcasted_iota(jnp.int32, (c, LANES), 1)
    lane2 = lax.broadcasted_iota(jnp.int32, (c2, LANES), 1)
    m0 = lane < HEAD_DIM
    m0_2 = lane2 < HEAD_DIM
    ri = lax.broadcasted_iota(jnp.int32, (c2, c2), 0)
    cj = lax.broadcasted_iota(jnp.int32, (c2, c2), 1)
    ti = jnp.bitwise_and(ri, c - 1)
    sj = jnp.bitwise_and(cj, c - 1)
    mask_z = (ti > sj) | ((ri >= c) & (ti == sj))
    left_top = lax.broadcasted_iota(jnp.int32, (c, c2), 1) < c
    blockdiag = jnp.bitwise_and(ri, c) == jnp.bitwise_and(cj, c)
    eye = (ri == cj)
    zero = jnp.zeros((c, LANES), F32)
    rows = lambda parts: jnp.concatenate(parts, axis=0)
    cols = lambda parts: jnp.concatenate(parts, axis=1)

    units = [(s, j) for s in range(n_chunks) for j in range(n_pairs)]

    def tile(arr, u):
        s, j = u
        return arr[s * c:(s + 1) * c, j * LANES:(j + 1) * LANES]

    at = [tile(at_all, u) for u in units]
    rt = [tile(rt_all, u) for u in units]
    vv = [tile(v, u) for u in units]

    z0, z1 = [], []
    for i, u in enumerate(units):
        xx = rows([at[i], rt[i]])
        bt, kt = tile(bt_all, u), tile(kt_all, u)
        z0.append(jnp.where(mask_z, _dot_nt(jnp.where(m0_2, xx, 0.0), rows([bt, kt])), 0.0))
        z1.append(jnp.where(mask_z, _dot_nt(jnp.where(m0_2, 0.0, xx), rows([kt, bt])), 0.0))

    pw = [rows([jnp.where(left_top, z0[i][0:c], 0.0), jnp.where(left_top, 0.0, z1[i][0:c])]) for i in range(len(units))]
    identity = jnp.where(eye, 1.0, 0.0)
    tinv = [identity + p for p in pw]
    for _ in range(5):
        pw = [_dot(p, p) for p in pw]
        tinv = [ti_ + _dot(ti_, p) for ti_, p in zip(tinv, pw)]

    akv = []
    for i in range(len(units)):
        v2 = rows([vv[i], vv[i]])
        akv.append(jnp.where(m0, _dot(jnp.where(left_top, 0.0, z0[i][0:c]), v2),
                             _dot(jnp.where(left_top, z1[i][0:c], 0.0), v2)))

    ah, uv = [], []
    for i in range(len(units)):
        wa = rows([jnp.where(m0, at[i], 0.0), jnp.where(m0, 0.0, at[i])])
        wv = rows([jnp.where(m0, akv[i], 0.0), jnp.where(m0, 0.0, akv[i])])
        tw = _dot(tinv[i], cols([wa, wv]))
        ah.append(tw[0:c, 0:LANES] + tw[c:c2, 0:LANES])
        uv.append(tw[0:c, LANES:] + tw[c:c2, LANES:])

    rh, ov, gmat, hmat = [], [], [], []
    for i, u in enumerate(units):
        rhs0 = cols([rows([ah[i], zero]), rows([uv[i], vv[i]])])
        rhs1 = cols([rows([zero, ah[i]]), rows([vv[i], uv[i]])])
        res0 = _dot(z0[i][c:c2], rhs0)
        res1 = _dot(z1[i][c:c2], rhs1)
        rh.append(rt[i] + jnp.where(m0, res0[:, 0:LANES], res1[:, 0:LANES]))
        ov.append(jnp.where(m0, res0[:, LANES:], res1[:, LANES:]))
        ypt = rows([tile(bp_all, u), tile(kp_all, u)]).T
        gh = _dot(ypt, rhs0)
        s, j = u
        p_end = jnp.exp(cl_ends[s][:, j * LANES:(j + 1) * LANES])
        gmat.append(jnp.where(blockdiag, gh[:, 0:LANES], 0.0)
                    + jnp.where(eye, jnp.broadcast_to(p_end, (c2, LANES)), 0.0))
        hmat.append(jnp.where(blockdiag, gh[:, LANES:], 0.0))

    states = [st_ref[n] for n in range(nb * n_pairs)]
    out_rows = []
    for s in range(n_chunks):
        out_pairs = []
        for j in range(n_pairs):
            i = s * n_pairs + j
            n = (s // chunks_per_seq) * n_pairs + j
            out_pairs.append(_dot(rh[i], states[n]) + ov[i])
            states[n] = _dot(gmat[i], states[n]) + hmat[i]
        out_rows.append(cols(out_pairs))
    for n in range(nb * n_pairs):
        st_ref[n] = states[n]

    o = rows(out_rows)
    mean = head_sum(o) * (1.0 / HEAD_DIM)
    dlt = o - mean
    var = head_sum(dlt * dlt) * (1.0 / HEAD_DIM)
    on = dlt * lax.rsqrt(var + GN_EPS) * lg_ref[...] + lb_ref[...]
    o_ref[...] = ((on + bonus) * g).astype(BF16).reshape(nb, tt, D_RWKV)

    for e in range(nb):
        _conv_tile(e, pc_ref, cw_ref, cb_ref, clg_ref, clb_ref, oc_ref, buf_ref, shift_ref, tt)


def _rwkv_conv_group(pr, pc, rwkv_params, conv_params, bd, layer, *, nb=4, tt=64):
    b, lp, _ = pr.shape
    assert b % nb == 0 and lp % tt == 0 and tt % RWKV_CHUNK == 0 and tt % CONV_SUB == 0
    n_pairs = D_RWKV // LANES
    idx = jnp.arange(nb * tt)
    tri = ((idx[:, None] >= idx[None, :]) & (idx[:, None] // RWKV_CHUNK == idx[None, :] // RWKV_CHUNK)).astype(BF16)

    def par(arr):
        return pl.BlockSpec((None,) + arr.shape[1:], lambda i, t: (layer,) + (0,) * (arr.ndim - 1))

    def const(arr):
        return pl.BlockSpec(arr.shape, lambda i, t: (0,) * arr.ndim)

    tile = lambda w: pl.BlockSpec((nb, tt, w), lambda i, t: (i, t, 0))
    return pl.pallas_call(
        functools.partial(_rwkv_conv_kernel, nb=nb, tt=tt),
        grid=(b // nb, lp // tt),
        in_specs=[tile(D_RWKV_IN)] + [par(p) for p in rwkv_params] + [const(bd), const(tri)]
        + [tile(2 * D_CONV)] + [par(p) for p in conv_params],
        out_specs=[tile(D_RWKV), tile(D_CONV)],
        out_shape=[jax.ShapeDtypeStruct((b, lp, D_RWKV), BF16), jax.ShapeDtypeStruct((b, lp, D_CONV), BF16)],
        scratch_shapes=[pltpu.VMEM((nb * n_pairs, LANES, LANES), F32), pltpu.VMEM((nb, 1, D_RWKV_IN), F32),
                        pltpu.VMEM((nb, tt + CONV_PAD, D_CONV), F32),
                        pltpu.VMEM((nb, 8, tt + 8 * ((CONV_WIDTH - 1) // 8), D_CONV), F32)],
        compiler_params=_cparams(2),
        name="rwkv_conv_group",
    )(pr, *rwkv_params, bd, tri, pc, *conv_params)


SB_TILE = 128
SB_GROUP_TILES = 8


def _sb_kernel(q_ref, k_ref, v_ref, g_ref, bd_ref, later_ref, o_ref, km_ref, vm_ref, *, nb):
    t = SB_TILE
    n_pairs = D_SB // LANES
    n_streams = nb * n_pairs
    qi = pl.program_id(1)

    @pl.when(qi == 0)
    def _():
        m0 = lax.broadcasted_iota(jnp.int32, (k_ref.shape[1], LANES), 1) < HEAD_DIM
        for p in range(n_streams):
            e, j = divmod(p, n_pairs)
            kj = k_ref[e, :, j * LANES:(j + 1) * LANES]
            vj = v_ref[e, :, j * LANES:(j + 1) * LANES]
            zero = jnp.zeros_like(kj)
            km_ref[2 * p] = jnp.where(m0, kj, zero)
            km_ref[2 * p + 1] = jnp.where(m0, zero, kj)
            vm_ref[2 * p] = jnp.where(m0, vj, zero)
            vm_ref[2 * p + 1] = jnp.where(m0, zero, vj)

    rowi = lax.broadcasted_iota(jnp.int32, (t, t), 0)
    coli = lax.broadcasted_iota(jnp.int32, (t, t), 1)
    causal = coli < rowi
    causal2 = jnp.concatenate([causal, causal], axis=1)
    later2 = later_ref[...]
    qb = [q_ref[p // n_pairs, :, (p % n_pairs) * LANES:(p % n_pairs + 1) * LANES] * (HEAD_DIM ** -0.5)
          for p in range(n_streams)]

    def both_heads(ref, kb, p):
        off = kb * t if isinstance(kb, int) else pl.multiple_of(kb * t, t)
        return jnp.concatenate([ref[2 * p, pl.ds(off, t), :], ref[2 * p + 1, pl.ds(off, t), :]], axis=0)

    def scores(kbs):
        return tuple(lax.dot_general(qb[p], both_heads(km_ref, kb, p), (((1,), (1,)), ((), ())),
                                     preferred_element_type=F32) for kb in kbs for p in range(n_streams))

    def group(kbs, zs, accs, carries, diagonal):
        units = [(kb, p) for kb in kbs for p in range(n_streams)]
        sps = [_softplus(z.astype(BF16)) for z in zs]
        if diagonal:
            sps = [jnp.where(causal2, sp, jnp.zeros_like(sp)) for sp in sps]
        locs = [jnp.dot(sp, later2, preferred_element_type=F32) for sp in sps]
        accs, carries = list(accs), list(carries)
        for i, (kb, p) in enumerate(units):
            halves = []
            for m in range(2):
                sl = slice(m * t, (m + 1) * t)
                att = jnp.exp(zs[i][:, sl] - locs[i][:, sl] - carries[2 * p + m])
                if diagonal:
                    att = jnp.where(causal, att, 0.0)
                halves.append(att.astype(BF16))
                carries[2 * p + m] = carries[2 * p + m] + locs[i][:, m * t:m * t + 1]
            accs[p] = accs[p] + jnp.dot(jnp.concatenate(halves, axis=1), both_heads(vm_ref, kb, p),
                                        preferred_element_type=F32)
        return tuple(accs), tuple(carries)

    state = (tuple(jnp.zeros((t, LANES), F32) for _ in range(n_streams)),
             tuple(jnp.zeros((t, 1), F32) for _ in range(2 * n_streams)))
    state = group([qi], scores([qi]), *state, True)

    def run(first, n_iters, width, state):
        def body(it, c):
            kbs = [first - width * it - w for w in range(width)]
            zs_next = scores(kbs[0:1])
            for w in range(width):
                zs = zs_next
                if w + 1 < width:
                    zs_next = scores(kbs[w + 1:w + 2])
                c = group(kbs[w:w + 1], zs, *c, False)
            return c
        return lax.fori_loop(0, n_iters, body, state)

    width = max(1, SB_GROUP_TILES // nb)
    n = jnp.right_shift(qi, width.bit_length() - 1)
    state = run(qi - 1, n, width, state)
    done = n * width
    while width > 1:
        width //= 2
        n = jnp.bitwise_and(jnp.right_shift(qi, width.bit_length() - 1), 1)
        state = run(qi - 1 - done, n, width, state)
        done = done + n * width
    accs = state[0]

    for p in range(n_streams):
        e, j = divmod(p, n_pairs)
        o = accs[p]
        hi, lo = _split2(o * o)
        ss = jnp.dot(hi, bd_ref[...], preferred_element_type=F32) + jnp.dot(lo, bd_ref[...], preferred_element_type=F32)
        o_ref[e, :, j * LANES:(j + 1) * LANES] = (o * lax.rsqrt(ss * (1.0 / HEAD_DIM) + RMS_EPS)
                                                  * g_ref[:, j * LANES:(j + 1) * LANES]).astype(BF16)


def _sb_group(ps, norm_g, bd128, layer, *, nb=2):
    b, lp, _ = ps.shape
    assert lp % SB_TILE == 0 and b % nb == 0
    n_pairs = D_SB // LANES
    idx = jnp.arange(2 * SB_TILE)
    later2 = ((idx[:, None] >= idx[None, :]) & (idx[:, None] // SB_TILE == idx[None, :] // SB_TILE)).astype(BF16)
    masked_copies = pltpu.VMEM((2 * nb * n_pairs, lp, LANES), BF16)
    return pl.pallas_call(
        functools.partial(_sb_kernel, nb=nb),
        grid=(b // nb, lp // SB_TILE),
        in_specs=[
            pl.BlockSpec((nb, SB_TILE, D_SB), lambda i, q: (i, q, 0)),
            pl.BlockSpec((nb, lp, D_SB), lambda i, q: (i, 0, 1)),
            pl.BlockSpec((nb, lp, D_SB), lambda i, q: (i, 0, 2)),
            pl.BlockSpec((None, 1, D_SB), lambda i, q: (layer, 0, 0)),
            pl.BlockSpec((LANES, LANES), lambda i, q: (0, 0)),
            pl.BlockSpec(later2.shape, lambda i, q: (0, 0)),
        ],
        out_specs=pl.BlockSpec((nb, SB_TILE, D_SB), lambda i, q: (i, q, 0)),
        out_shape=jax.ShapeDtypeStruct((b, lp, D_SB), BF16),
        scratch_shapes=[masked_copies, masked_copies],
        compiler_params=_cparams(2),
        name="sb_group",
    )(ps, ps, ps, norm_g, bd128, later2)


def _block_diag_ones(n):
    i = jnp.arange(n) // HEAD_DIM
    return (i[:, None] == i[None, :]).astype(BF16)


@jax.jit
def _trunk(x, meta, ffn1_norm, ffn1_w13, ffn1_w2, mix_norm, w_in, conv_w, conv_b, conv_ln_g, conv_ln_b, rwkv_mu,
           rwkv_w0, rwkv_wB, rwkv_a0, rwkv_aB, rwkv_gB, rwkv_kk, rwkv_ka, rwkv_rk, rwkv_ln_g, rwkv_ln_b, sb_norm,
           w_out, ffn2_norm, ffn2_w13, ffn2_w2, final_norm):
    bsz, seq, d = x.shape
    depth = w_in.shape[0]
    l_real = N_META + seq
    lp = -(-l_real // SEQ_ALIGN) * SEQ_ALIGN
    meta_b = jnp.broadcast_to(meta.astype(x.dtype)[None], (bsz, N_META, d))
    h = jnp.concatenate([meta_b, x, jnp.zeros((bsz, lp - l_real, d), x.dtype)], axis=1).reshape(bsz * lp, d)

    row3 = lambda p: p.reshape(depth, 1, -1)
    ffn1_w13, ffn1_w2, ffn2_w13, ffn2_w2 = (w.astype(BF16) for w in (ffn1_w13, ffn1_w2, ffn2_w13, ffn2_w2))
    w_in, w_out = w_in.astype(BF16), w_out.astype(BF16)
    ffn1_norm, mix_norm, ffn2_norm, sb_norm = row3(ffn1_norm), row3(mix_norm), row3(ffn2_norm), row3(sb_norm)
    conv_b, conv_ln_g, conv_ln_b = row3(conv_b), row3(conv_ln_g), row3(conv_ln_b)
    rwkv_vecs = [row3(p) for p in (rwkv_mu, rwkv_w0, rwkv_a0, rwkv_kk, rwkv_ka, rwkv_rk, rwkv_ln_g, rwkv_ln_b)]
    mu, w0, a0, k_k, k_a, r_k, ln_g, ln_b = rwkv_vecs
    wb_ext = jnp.pad(rwkv_wB, ((0, 0), (0, AAA_LORA), (0, 0))).astype(BF16)
    ab_ext = jnp.pad(rwkv_aB, ((0, 0), (DECAY_LORA, 0), (0, 0))).astype(BF16)
    gb = rwkv_gB.astype(BF16)
    bd_rwkv = _block_diag_ones(D_RWKV)
    bd_pair = _block_diag_ones(LANES)

    for l in range(depth):
        h = _ffn(h, ffn1_norm, ffn1_w13, ffn1_w2, l)
        pc, pr, ps = _inproj(h, mix_norm, w_in, l)
        yr, yc = _rwkv_conv_group(pr.reshape(bsz, lp, -1), pc.reshape(bsz, lp, -1),
                                  (mu, w0, wb_ext, a0, ab_ext, gb, k_k, k_a, r_k, ln_g, ln_b),
                                  (conv_w, conv_b, conv_ln_g, conv_ln_b), bd_rwkv, l)
        ys = _sb_group(ps.reshape(bsz, lp, -1), sb_norm, bd_pair, l)
        mix = (yc.reshape(bsz * lp, -1), yr.reshape(bsz * lp, -1), ys.reshape(bsz * lp, -1), w_out)
        h = _ffn(h, ffn2_norm, ffn2_w13, ffn2_w2, l, mix=mix,
                 final_g=final_norm.reshape(1, d) if l == depth - 1 else None)
    out = h
    return out.reshape(bsz, lp, d)[:, N_META:l_real]


def kernel(x, meta, ffn1_norm, ffn1_w13, ffn1_w2, mix_norm, w_in, conv_w, conv_b, conv_ln_g, conv_ln_b, rwkv_mu,
           rwkv_w0, rwkv_wB, rwkv_a0, rwkv_aB, rwkv_gB, rwkv_kk, rwkv_ka, rwkv_rk, rwkv_ln_g, rwkv_ln_b, sb_norm,
           w_out, ffn2_norm, ffn2_w13, ffn2_w2, final_norm):
    return _trunk(x, meta, ffn1_norm, ffn1_w13, ffn1_w2, mix_norm, w_in, conv_w, conv_b, conv_ln_g, conv_ln_b,
                  rwkv_mu, rwkv_w0, rwkv_wB, rwkv_a0, rwkv_aB, rwkv_gB, rwkv_kk, rwkv_ka, rwkv_rk, rwkv_ln_g,
                  rwkv_ln_b, sb_norm, w_out, ffn2_norm, ffn2_w13, ffn2_w2, final_norm)
```

```python
import functools

import jax
import jax.numpy as jnp
from jax import lax
from jax.experimental import pallas as pl
from jax.experimental.pallas import tpu as pltpu

F32 = jnp.float32
BF16 = jnp.bfloat16

N_META = 16
HEAD_DIM = 64
D_CONV = 256
CONV_WIDTH = 31
D_RWKV = 384
D_SB = 384
DECAY_LORA = 64
AAA_LORA = 64
GATE_LORA = 128
D_RWKV_IN = 3 * D_RWKV + DECAY_LORA + AAA_LORA + GATE_LORA
RMS_EPS = 1e-6
LN_EPS = 1e-5
GN_EPS = 64e-5

LANES = 128
SEQ_ALIGN = 128
RWKV_CHUNK = 64
VMEM_LIMIT = 56 * 1024 * 1024


def _cparams(n_axes):
    return pltpu.CompilerParams(dimension_semantics=("arbitrary",) * n_axes, vmem_limit_bytes=VMEM_LIMIT)


def _dot(a, b):
    return jnp.dot(a.astype(BF16), b.astype(BF16), preferred_element_type=F32)


def _dot_nt(a, b):
    return lax.dot_general(a.astype(BF16), b.astype(BF16), (((1,), (1,)), ((), ())), preferred_element_type=F32)


def _split3(x):
    h1 = x.astype(BF16)
    r1 = x - h1.astype(F32)
    h2 = r1.astype(BF16)
    return h1, h2, (r1 - h2.astype(F32)).astype(BF16)


def _sigmoid(x):
    return 0.5 * jnp.tanh(0.5 * x) + 0.5


def _softplus(x):
    return jnp.maximum(x, 0.0) + jnp.log(1.0 + jnp.exp(-jnp.abs(x)))


def _rms_norm_rows(x, g):
    ms = jnp.mean(x * x, axis=-1, keepdims=True)
    return x * lax.rsqrt(ms + RMS_EPS) * g


def _ffn_kernel(h_ref, yc_ref, yr_ref, ys_ref, wo_ref, g_ref, w13_ref, w2_ref, fg_ref, o_ref, act_ref, *,
                d_ff, tf, mix, final):
    hin = h_ref[...]
    if mix:
        r0 = D_CONV
        r1 = D_CONV + D_RWKV
        hin = hin + jnp.dot(yc_ref[...], wo_ref[0:r0, :], preferred_element_type=F32)
        hin = hin + jnp.dot(yr_ref[...], wo_ref[r0:r1, :], preferred_element_type=F32)
        hin = hin + jnp.dot(ys_ref[...], wo_ref[r1:, :], preferred_element_type=F32)
    o_ref[...] = hin
    xn = _rms_norm_rows(hin, g_ref[...]).astype(BF16)
    for c0 in range(0, d_ff, tf):
        gate = jnp.dot(xn, w13_ref[:, c0:c0 + tf], preferred_element_type=F32)
        up = jnp.dot(xn, w13_ref[:, d_ff + c0:d_ff + c0 + tf], preferred_element_type=F32)
        act_ref[:, c0:c0 + tf] = (gate * _sigmoid(gate) * up).astype(BF16)
    out = o_ref[...] + 0.5 * jnp.dot(act_ref[...], w2_ref[...], preferred_element_type=F32)
    if final:
        out = _rms_norm_rows(out, fg_ref[...])
    o_ref[...] = out


def _ffn(h2d, norm_g, w13, w2, layer, *, mix=None, final_g=None, tm=1024, tf=256):
    m, d = h2d.shape
    d_ff = w2.shape[1]
    assert m % tm == 0 and d_ff % tf == 0
    resident = dict(pipeline_mode=pl.Buffered(1))
    rows = lambda w: pl.BlockSpec((tm, w), lambda i: (i, 0))
    if mix is None:
        mix_args = [jnp.zeros((tm, w), BF16) for w in (D_CONV, D_RWKV, D_SB)] + [jnp.zeros((1, 16, d), BF16)]
        mix_specs = [pl.BlockSpec((tm, w), lambda i: (0, 0)) for w in (D_CONV, D_RWKV, D_SB)]
        wo_layer = 0
    else:
        mix_args = list(mix)
        mix_specs = [rows(D_CONV), rows(D_RWKV), rows(D_SB)]
        wo_layer = layer
    mix_specs.append(pl.BlockSpec((None,) + mix_args[3].shape[1:], lambda i: (wo_layer, 0, 0), **resident))
    fg = jnp.zeros((1, d), F32) if final_g is None else final_g
    return pl.pallas_call(
        functools.partial(_ffn_kernel, d_ff=d_ff, tf=tf, mix=mix is not None, final=final_g is not None),
        grid=(m // tm,),
        in_specs=[rows(d)] + mix_specs + [
            pl.BlockSpec((None, 1, d), lambda i: (layer, 0, 0)),
            pl.BlockSpec((None, d, 2 * d_ff), lambda i: (layer, 0, 0), **resident),
            pl.BlockSpec((None, d_ff, d), lambda i: (layer, 0, 0), **resident),
            pl.BlockSpec((1, d), lambda i: (0, 0)),
        ],
        out_specs=rows(d),
        out_shape=jax.ShapeDtypeStruct((m, d), F32),
        scratch_shapes=[pltpu.VMEM((tm, d_ff), BF16)],
        compiler_params=_cparams(1),
        name="ffn",
    )(h2d, *mix_args, norm_g, w13, w2, fg)


def _inproj_kernel(h_ref, g_ref, w_ref, pc_ref, pr_ref, ps_ref):
    xn = _rms_norm_rows(h_ref[...], g_ref[...]).astype(BF16)
    c0 = 2 * D_CONV
    c1 = c0 + D_RWKV_IN
    pc_ref[...] = jnp.dot(xn, w_ref[:, 0:c0], preferred_element_type=F32)
    pr_ref[...] = jnp.dot(xn, w_ref[:, c0:c1], preferred_element_type=F32)
    ps_ref[...] = jnp.dot(xn, w_ref[:, c1:], preferred_element_type=F32).astype(BF16)


def _inproj(h2d, norm_g, w_in, layer, *, tm=1024):
    m, d = h2d.shape
    assert m % tm == 0
    d_in = w_in.shape[2]
    widths = (2 * D_CONV, D_RWKV_IN, 3 * D_SB)
    return pl.pallas_call(
        _inproj_kernel,
        grid=(m // tm,),
        in_specs=[
            pl.BlockSpec((tm, d), lambda i: (i, 0)),
            pl.BlockSpec((None, 1, d), lambda i: (layer, 0, 0)),
            pl.BlockSpec((None, d, d_in), lambda i: (layer, 0, 0), pipeline_mode=pl.Buffered(1)),
        ],
        out_specs=[pl.BlockSpec((tm, w), lambda i: (i, 0)) for w in widths],
        out_shape=[jax.ShapeDtypeStruct((m, w), dt) for w, dt in zip(widths, (F32, F32, BF16))],
        compiler_params=_cparams(1),
        name="inproj",
    )(h2d, norm_g, w_in)


CONV_PAD = 32
CONV_SUB = 64


def _conv_tile(e, p_ref, w_ref, b_ref, lg_ref, lb_ref, o_ref, buf_ref, shift_ref, tt):
    val = p_ref[e, :, 0:D_CONV]
    gate = p_ref[e, :, D_CONV:2 * D_CONV]
    buf_ref[e, CONV_PAD:CONV_PAD + tt, :] = val * _sigmoid(gate)
    first = CONV_PAD - (CONV_WIDTH - 1)
    for b in range(8):
        span = tt + 8 * ((CONV_WIDTH - 1 - b) // 8)
        shift_ref[e, b, 0:span, :] = buf_ref[e, first + b:first + b + span, :]
    for s in range(tt // CONV_SUB):
        acc = jnp.zeros((CONV_SUB, D_CONV), F32)
        for j in range(CONV_WIDTH):
            r0 = s * CONV_SUB + 8 * (j // 8)
            acc = acc + w_ref[j:j + 1, :] * shift_ref[e, j % 8, r0:r0 + CONV_SUB, :]
        y = acc + b_ref[...]
        mu = jnp.mean(y, axis=-1, keepdims=True)
        dlt = y - mu
        var = jnp.mean(dlt * dlt, axis=-1, keepdims=True)
        yn = dlt * lax.rsqrt(var + LN_EPS) * lg_ref[...] + lb_ref[...]
        o_ref[e, s * CONV_SUB:(s + 1) * CONV_SUB, :] = (yn * _sigmoid(yn)).astype(BF16)
    buf_ref[e, 0:CONV_PAD, :] = buf_ref[e, tt:tt + CONV_PAD, :]


def _rwkv_conv_kernel(p_ref, mu_ref, w0_ref, wb_ref, a0_ref, ab_ref, gb_ref, kk_ref, ka_ref, rk_ref, lg_ref, lb_ref,
                      bd_ref, tri_ref, pc_ref, cw_ref, cb_ref, clg_ref, clb_ref, o_ref, oc_ref,
                      st_ref, carry_ref, buf_ref, shift_ref, *, nb, tt):
    c = RWKV_CHUNK
    c2 = 2 * c
    n_pairs = D_RWKV // LANES
    chunks_per_seq = tt // c
    n_chunks = nb * chunks_per_seq
    t = pl.program_id(1)

    @pl.when(t == 0)
    def _():
        st_ref[...] = jnp.zeros_like(st_ref)
        carry_ref[...] = jnp.zeros_like(carry_ref)
        buf_ref[:, 0:CONV_PAD, :] = jnp.zeros((nb, CONV_PAD, D_CONV), F32)


    x = p_ref[...].reshape(nb * tt, D_RWKV_IN)
    row = lax.broadcasted_iota(jnp.int32, x.shape, 0)
    prev = pltpu.roll(x, 1, 0)
    for e in range(nb):
        prev = jnp.where(row == e * tt, carry_ref[e], prev)
        carry_ref[e] = x[(e + 1) * tt - 1:(e + 1) * tt, :]
    xs = x + mu_ref[...] * (prev - x)

    r = xs[:, 0:D_RWKV]
    k = xs[:, D_RWKV:2 * D_RWKV]
    v = xs[:, 2 * D_RWKV:3 * D_RWKV]
    lora_in = xs[:, 3 * D_RWKV:3 * D_RWKV + DECAY_LORA + AAA_LORA]
    gd = xs[:, 3 * D_RWKV + DECAY_LORA + AAA_LORA:]

    bd = bd_ref[...]

    def head_sum(z):
        zb = z.astype(BF16)
        return jnp.concatenate([jnp.dot(zb[:, j * LANES:(j + 1) * LANES], bd, preferred_element_type=F32)
                                for j in range(n_pairs)], axis=1)

    w_log = -_softplus(-(w0_ref[...] + _dot(jnp.tanh(lora_in), wb_ref[...]))) - 0.5
    logw = -jnp.exp(w_log)
    a = _sigmoid(a0_ref[...] + _dot(lora_in, ab_ref[...]))
    g = _dot(_sigmoid(gd), gb_ref[...])
    kk = k * kk_ref[...]
    kk = kk * lax.rsqrt(jnp.maximum(head_sum(kk * kk), 1e-24))
    kq = k * (1.0 + (a - 1.0) * ka_ref[...])
    na = -kk
    b = kk * a
    bonus = head_sum(r * kq * rk_ref[...]) * v

    tri = tri_ref[...]
    l1, l2, l3 = _split3(logw)
    cl = (jnp.dot(tri, l1, preferred_element_type=F32) + jnp.dot(tri, l2, preferred_element_type=F32)
          + jnp.dot(tri, l3, preferred_element_type=F32))
    cl_ends = [cl[(s + 1) * c - 1:(s + 1) * c, :] for s in range(n_chunks)]
    cl_end = jnp.concatenate([jnp.broadcast_to(e, (c, D_RWKV)) for e in cl_ends], axis=0)
    e_pos = jnp.exp(cl)
    e_neg = jnp.exp(-cl)
    e_end = jnp.exp(cl_end - cl)
    at_all = jnp.exp(cl - logw) * na
    bt_all = b * e_neg
    kt_all = kq * e_neg
    rt_all = r * e_pos
    bp_all = b * e_end
    kp_all = kq * e_end

    lane = lax.broadcasted_iota(jnp.int32, (c, LANES), 1)
    lane2 = lax.broadcasted_iota(jnp.int32, (c2, LANES), 1)
    m0 = lane < HEAD_DIM
    m0_2 = lane2 < HEAD_DIM
    ri = lax.broadcasted_iota(jnp.int32, (c2, c2), 0)
    cj = lax.broadcasted_iota(jnp.int32, (c2, c2), 1)
    ti = jnp.bitwise_and(ri, c - 1)
    sj = jnp.bitwise_and(cj, c - 1)
    mask_z = (ti > sj) | ((ri >= c) & (ti == sj))
    left_top = lax.broadcasted_iota(jnp.int32, (c, c2), 1) < c
    blockdiag = jnp.bitwise_and(ri, c) == jnp.bitwise_and(cj, c)
    eye = (ri == cj)
    zero = jnp.zeros((c, LANES), F32)
    rows = lambda parts: jnp.concatenate(parts, axis=0)
    cols = lambda parts: jnp.concatenate(parts, axis=1)

    units = [(s, j) for s in range(n_chunks) for j in range(n_pairs)]

    def tile(arr, u):
        s, j = u
        return arr[s * c:(s + 1) * c, j * LANES:(j + 1) * LANES]

    at = [tile(at_all, u) for u in units]
    rt = [tile(rt_all, u) for u in units]
    vv = [tile(v, u) for u in units]

    z0, z1 = [], []
    for i, u in enumerate(units):
        xx = rows([at[i], rt[i]])
        bt, kt = tile(bt_all, u), tile(kt_all, u)
        z0.append(jnp.where(mask_z, _dot_nt(jnp.where(m0_2, xx, 0.0), rows([bt, kt])), 0.0))
        z1.append(jnp.where(mask_z, _dot_nt(jnp.where(m0_2, 0.0, xx), rows([kt, bt])), 0.0))

    pw = [rows([jnp.where(left_top, z0[i][0:c], 0.0), jnp.where(left_top, 0.0, z1[i][0:c])]) for i in range(len(units))]
    identity = jnp.where(eye, 1.0, 0.0)
    tinv = [identity + p for p in pw]
    for _ in range(5):
        pw = [_dot(p, p) for p in pw]
        tinv = [ti_ + _dot(ti_, p) for ti_, p in zip(tinv, pw)]

    akv = []
    for i in range(len(units)):
        v2 = rows([vv[i], vv[i]])
        akv.append(jnp.where(m0, _dot(jnp.where(left_top, 0.0, z0[i][0:c]), v2),
                             _dot(jnp.where(left_top, z1[i][0:c], 0.0), v2)))

    ah, uv = [], []
    for i in range(len(units)):
        wa = rows([jnp.where(m0, at[i], 0.0), jnp.where(m0, 0.0, at[i])])
        wv = rows([jnp.where(m0, akv[i], 0.0), jnp.where(m0, 0.0, akv[i])])
        tw = _dot(tinv[i], cols([wa, wv]))
        ah.append(tw[0:c, 0:LANES] + tw[c:c2, 0:LANES])
        uv.append(tw[0:c, LANES:] + tw[c:c2, LANES:])

    rh, ov, gmat, hmat = [], [], [], []
    for i, u in enumerate(units):
        rhs0 = cols([rows([ah[i], zero]), rows([uv[i], vv[i]])])
        rhs1 = cols([rows([zero, ah[i]]), rows([vv[i], uv[i]])])
        res0 = _dot(z0[i][c:c2], rhs0)
        res1 = _dot(z1[i][c:c2], rhs1)
        rh.append(rt[i] + jnp.where(m0, res0[:, 0:LANES], res1[:, 0:LANES]))
        ov.append(jnp.where(m0, res0[:, LANES:], res1[:, LANES:]))
        ypt = rows([tile(bp_all, u), tile(kp_all, u)]).T
        gh = _dot(ypt, rhs0)
        s, j = u
        p_end = jnp.exp(cl_ends[s][:, j * LANES:(j + 1) * LANES])
        gmat.append(jnp.where(blockdiag, gh[:, 0:LANES], 0.0)
                    + jnp.where(eye, jnp.broadcast_to(p_end, (c2, LANES)), 0.0))
        hmat.append(jnp.where(blockdiag, gh[:, LANES:], 0.0))

    states = [st_ref[n] for n in range(nb * n_pairs)]
    out_rows = []
    for s in range(n_chunks):
        out_pairs = []
        for j in range(n_pairs):
            i = s * n_pairs + j
            n = (s // chunks_per_seq) * n_pairs + j
            out_pairs.append(_dot(rh[i], states[n]) + ov[i])
            states[n] = _dot(gmat[i], states[n]) + hmat[i]
        out_rows.append(cols(out_pairs))
    for n in range(nb * n_pairs):
        st_ref[n] = states[n]

    o = rows(out_rows)
    mean = head_sum(o) * (1.0 / HEAD_DIM)
    dlt = o - mean
    var = head_sum(dlt * dlt) * (1.0 / HEAD_DIM)
    on = dlt * lax.rsqrt(var + GN_EPS) * lg_ref[...] + lb_ref[...]
    o_ref[...] = ((on + bonus) * g).astype(BF16).reshape(nb, tt, D_RWKV)

    for e in range(nb):
        _conv_tile(e, pc_ref, cw_ref, cb_ref, clg_ref, clb_ref, oc_ref, buf_ref, shift_ref, tt)


def _rwkv_conv_group(pr, pc, rwkv_params, conv_params, bd, layer, *, nb=4, tt=64):
    b, lp, _ = pr.shape
    assert b % nb == 0 and lp % tt == 0 and tt % RWKV_CHUNK == 0 and tt % CONV_SUB == 0
    n_pairs = D_RWKV // LANES
    idx = jnp.arange(nb * tt)
    tri = ((idx[:, None] >= idx[None, :]) & (idx[:, None] // RWKV_CHUNK == idx[None, :] // RWKV_CHUNK)).astype(BF16)

    def par(arr):
        return pl.BlockSpec((None,) + arr.shape[1:], lambda i, t: (layer,) + (0,) * (arr.ndim - 1))

    def const(arr):
        return pl.BlockSpec(arr.shape, lambda i, t: (0,) * arr.ndim)

    tile = lambda w: pl.BlockSpec((nb, tt, w), lambda i, t: (i, t, 0))
    return pl.pallas_call(
        functools.partial(_rwkv_conv_kernel, nb=nb, tt=tt),
        grid=(b // nb, lp // tt),
        in_specs=[tile(D_RWKV_IN)] + [par(p) for p in rwkv_params] + [const(bd), const(tri)]
        + [tile(2 * D_CONV)] + [par(p) for p in conv_params],
        out_specs=[tile(D_RWKV), tile(D_CONV)],
        out_shape=[jax.ShapeDtypeStruct((b, lp, D_RWKV), BF16), jax.ShapeDtypeStruct((b, lp, D_CONV), BF16)],
        scratch_shapes=[pltpu.VMEM((nb * n_pairs, LANES, LANES), F32), pltpu.VMEM((nb, 1, D_RWKV_IN), F32),
                        pltpu.VMEM((nb, tt + CONV_PAD, D_CONV), F32),
                        pltpu.VMEM((nb, 8, tt + 8 * ((CONV_WIDTH - 1) // 8), D_CONV), F32)],
        compiler_params=_cparams(2),
        name="rwkv_conv_group",
    )(pr, *rwkv_params, bd, tri, pc, *conv_params)


SB_TILE = 128
SB_GROUP_TILES = 8


def _sb_kernel(q_ref, k_ref, v_ref, g_ref, bd_ref, later_ref, o_ref, km_ref, vm_ref, *, nb):
    t = SB_TILE
    n_pairs = D_SB // LANES
    n_streams = nb * n_pairs
    qi = pl.program_id(1)

    @pl.when(qi == 0)
    def _():
        m0 = lax.broadcasted_iota(jnp.int32, (k_ref.shape[1], LANES), 1) < HEAD_DIM
        for p in range(n_streams):
            e, j = divmod(p, n_pairs)
            kj = k_ref[e, :, j * LANES:(j + 1) * LANES]
            vj = v_ref[e, :, j * LANES:(j + 1) * LANES]
            zero = jnp.zeros_like(kj)
            km_ref[2 * p] = jnp.where(m0, kj, zero)
            km_ref[2 * p + 1] = jnp.where(m0, zero, kj)
            vm_ref[2 * p] = jnp.where(m0, vj, zero)
            vm_ref[2 * p + 1] = jnp.where(m0, zero, vj)

    rowi = lax.broadcasted_iota(jnp.int32, (t, t), 0)
    coli = lax.broadcasted_iota(jnp.int32, (t, t), 1)
    causal = coli < rowi
    causal2 = jnp.concatenate([causal, causal], axis=1)
    later2 = later_ref[...]
    qb = [q_ref[p // n_pairs, :, (p % n_pairs) * LANES:(p % n_pairs + 1) * LANES] * (HEAD_DIM ** -0.5)
          for p in range(n_streams)]

    def both_heads(ref, kb, p):
        off = kb * t if isinstance(kb, int) else pl.multiple_of(kb * t, t)
        return jnp.concatenate([ref[2 * p, pl.ds(off, t), :], ref[2 * p + 1, pl.ds(off, t), :]], axis=0)

    def scores(kbs):
        return tuple(lax.dot_general(qb[p], both_heads(km_ref, kb, p), (((1,), (1,)), ((), ())),
                                     preferred_element_type=F32) for kb in kbs for p in range(n_streams))

    def group(kbs, zs, accs, carries, diagonal):
        units = [(kb, p) for kb in kbs for p in range(n_streams)]
        sps = [_softplus(z.astype(BF16)) for z in zs]
        if diagonal:
            sps = [jnp.where(causal2, sp, jnp.zeros_like(sp)) for sp in sps]
        locs = [jnp.dot(sp, later2, preferred_element_type=F32) for sp in sps]
        accs, carries = list(accs), list(carries)
        for i, (kb, p) in enumerate(units):
            halves = []
            for m in range(2):
                sl = slice(m * t, (m + 1) * t)
                att = jnp.exp(zs[i][:, sl] - locs[i][:, sl] - carries[2 * p + m])
                if diagonal:
                    att = jnp.where(causal, att, 0.0)
                halves.append(att.astype(BF16))
                carries[2 * p + m] = carries[2 * p + m] + locs[i][:, m * t:m * t + 1]
            accs[p] = accs[p] + jnp.dot(jnp.concatenate(halves, axis=1), both_heads(vm_ref, kb, p),
                                        preferred_element_type=F32)
        return tuple(accs), tuple(carries)

    state = (tuple(jnp.zeros((t, LANES), F32) for _ in range(n_streams)),
             tuple(jnp.zeros((t, 1), F32) for _ in range(2 * n_streams)))
    state = group([qi], scores([qi]), *state, True)

    def run(first, n_iters, width, state):
        def body(it, c):
            kbs = [first - width * it - w for w in range(width)]
            zs_next = scores(kbs[0:1])
            for w in range(width):
                zs = zs_next
                if w + 1 < width:
                    zs_next = scores(kbs[w + 1:w + 2])
                c = group(kbs[w:w + 1], zs, *c, False)
            return c
        return lax.fori_loop(0, n_iters, body, state)

    width = max(1, SB_GROUP_TILES // nb)
    n = jnp.right_shift(qi, width.bit_length() - 1)
    state = run(qi - 1, n, width, state)
    done = n * width
    while width > 1:
        width //= 2
        n = jnp.bitwise_and(jnp.right_shift(qi, width.bit_length() - 1), 1)
        state = run(qi - 1 - done, n, width, state)
        done = done + n * width
    accs = state[0]

    for p in range(n_streams):
        e, j = divmod(p, n_pairs)
        o = accs[p]
        ss = jnp.dot((o * o).astype(BF16), bd_ref[...], preferred_element_type=F32)
        o_ref[e, :, j * LANES:(j + 1) * LANES] = (o * lax.rsqrt(ss * (1.0 / HEAD_DIM) + RMS_EPS)
                                                  * g_ref[:, j * LANES:(j + 1) * LANES]).astype(BF16)


def _sb_group(ps, norm_g, bd128, layer, *, nb=2):
    b, lp, _ = ps.shape
    assert lp % SB_TILE == 0 and b % nb == 0
    n_pairs = D_SB // LANES
    idx = jnp.arange(2 * SB_TILE)
    later2 = ((idx[:, None] >= idx[None, :]) & (idx[:, None] // SB_TILE == idx[None, :] // SB_TILE)).astype(BF16)
    masked_copies = pltpu.VMEM((2 * nb * n_pairs, lp, LANES), BF16)
    return pl.pallas_call(
        functools.partial(_sb_kernel, nb=nb),
        grid=(b // nb, lp // SB_TILE),
        in_specs=[
            pl.BlockSpec((nb, SB_TILE, D_SB), lambda i, q: (i, q, 0)),
            pl.BlockSpec((nb, lp, D_SB), lambda i, q: (i, 0, 1)),
            pl.BlockSpec((nb, lp, D_SB), lambda i, q: (i, 0, 2)),
            pl.BlockSpec((None, 1, D_SB), lambda i, q: (layer, 0, 0)),
            pl.BlockSpec((LANES, LANES), lambda i, q: (0, 0)),
            pl.BlockSpec(later2.shape, lambda i, q: (0, 0)),
        ],
        out_specs=pl.BlockSpec((nb, SB_TILE, D_SB), lambda i, q: (i, q, 0)),
        out_shape=jax.ShapeDtypeStruct((b, lp, D_SB), BF16),
        scratch_shapes=[masked_copies, masked_copies],
        compiler_params=_cparams(2),
        name="sb_group",
    )(ps, ps, ps, norm_g, bd128, later2)


def _block_diag_ones(n):
    i = jnp.arange(n) // HEAD_DIM
    return (i[:, None] == i[None, :]).astype(BF16)


@jax.jit
def _trunk(x, meta, ffn1_norm, ffn1_w13, ffn1_w2, mix_norm, w_in, conv_w, conv_b, conv_ln_g, conv_ln_b, rwkv_mu,
           rwkv_w0, rwkv_wB, rwkv_a0, rwkv_aB, rwkv_gB, rwkv_kk, rwkv_ka, rwkv_rk, rwkv_ln_g, rwkv_ln_b, sb_norm,
           w_out, ffn2_norm, ffn2_w13, ffn2_w2, final_norm):
    bsz, seq, d = x.shape
    depth = w_in.shape[0]
    l_real = N_META + seq
    lp = -(-l_real // SEQ_ALIGN) * SEQ_ALIGN
    meta_b = jnp.broadcast_to(meta.astype(x.dtype)[None], (bsz, N_META, d))
    h = jnp.concatenate([meta_b, x, jnp.zeros((bsz, lp - l_real, d), x.dtype)], axis=1).reshape(bsz * lp, d)

    row3 = lambda p: p.reshape(depth, 1, -1)
    ffn1_w13, ffn1_w2, ffn2_w13, ffn2_w2 = (w.astype(BF16) for w in (ffn1_w13, ffn1_w2, ffn2_w13, ffn2_w2))
    w_in, w_out = w_in.astype(BF16), w_out.astype(BF16)
    ffn1_norm, mix_norm, ffn2_norm, sb_norm = row3(ffn1_norm), row3(mix_norm), row3(ffn2_norm), row3(sb_norm)
    conv_b, conv_ln_g, conv_ln_b = row3(conv_b), row3(conv_ln_g), row3(conv_ln_b)
    rwkv_vecs = [row3(p) for p in (rwkv_mu, rwkv_w0, rwkv_a0, rwkv_kk, rwkv_ka, rwkv_rk, rwkv_ln_g, rwkv_ln_b)]
    mu, w0, a0, k_k, k_a, r_k, ln_g, ln_b = rwkv_vecs
    wb_ext = jnp.pad(rwkv_wB, ((0, 0), (0, AAA_LORA), (0, 0))).astype(BF16)
    ab_ext = jnp.pad(rwkv_aB, ((0, 0), (DECAY_LORA, 0), (0, 0))).astype(BF16)
    gb = rwkv_gB.astype(BF16)
    bd_pair = _block_diag_ones(LANES)

    for l in range(depth):
        h = _ffn(h, ffn1_norm, ffn1_w13, ffn1_w2, l)
        pc, pr, ps = _inproj(h, mix_norm, w_in, l)
        yr, yc = _rwkv_conv_group(pr.reshape(bsz, lp, -1), pc.reshape(bsz, lp, -1),
                                  (mu, w0, wb_ext, a0, ab_ext, gb, k_k, k_a, r_k, ln_g, ln_b),
                                  (conv_w, conv_b, conv_ln_g, conv_ln_b), bd_pair, l)
        ys = _sb_group(ps.reshape(bsz, lp, -1), sb_norm, bd_pair, l)
        mix = (yc.reshape(bsz * lp, -1), yr.reshape(bsz * lp, -1), ys.reshape(bsz * lp, -1), w_out)
        h = _ffn(h, ffn2_norm, ffn2_w13, ffn2_w2, l, mix=mix,
                 final_g=final_norm.reshape(1, d) if l == depth - 1 else None)
    out = h
    return out.reshape(bsz, lp, d)[:, N_META:l_real]


def kernel(x, meta, ffn1_norm, ffn1_w13, ffn1_w2, mix_norm, w_in, conv_w, conv_b, conv_ln_g, conv_ln_b, rwkv_mu,
           rwkv_w0, rwkv_wB, rwkv_a0, rwkv_aB, rwkv_gB, rwkv_kk, rwkv_ka, rwkv_rk, rwkv_ln_g, rwkv_ln_b, sb_norm,
           w_out, ffn2_norm, ffn2_w13, ffn2_w2, final_norm):
    return _trunk(x, meta, ffn1_norm, ffn1_w13, ffn1_w2, mix_norm, w_in, conv_w, conv_b, conv_ln_g, conv_ln_b,
                  rwkv_mu, rwkv_w0, rwkv_wB, rwkv_a0, rwkv_aB, rwkv_gB, rwkv_kk, rwkv_ka, rwkv_rk, rwkv_ln_g,
                  rwkv_ln_b, sb_norm, w_out, ffn2_norm, ffn2_w13, ffn2_w2, final_norm)
```

```python
import functools

import jax
import jax.numpy as jnp
from jax import lax
from jax.experimental import pallas as pl
from jax.experimental.pallas import tpu as pltpu

F32 = jnp.float32
BF16 = jnp.bfloat16

N_META = 16
HEAD_DIM = 64
D_CONV = 256
CONV_WIDTH = 31
D_RWKV = 384
D_SB = 384
DECAY_LORA = 64
AAA_LORA = 64
GATE_LORA = 128
D_RWKV_IN = 3 * D_RWKV + DECAY_LORA + AAA_LORA + GATE_LORA
RMS_EPS = 1e-6
LN_EPS = 1e-5
GN_EPS = 64e-5

LANES = 128
SEQ_ALIGN = 128
RWKV_CHUNK = 64
VMEM_LIMIT = 56 * 1024 * 1024


def _cparams(n_axes):
    return pltpu.CompilerParams(dimension_semantics=("arbitrary",) * n_axes, vmem_limit_bytes=VMEM_LIMIT)


def _dot(a, b):
    return jnp.dot(a.astype(BF16), b.astype(BF16), preferred_element_type=F32)


def _dot_nt(a, b):
    return lax.dot_general(a.astype(BF16), b.astype(BF16), (((1,), (1,)), ((), ())), preferred_element_type=F32)


def _split3(x):
    h1 = x.astype(BF16)
    r1 = x - h1.astype(F32)
    h2 = r1.astype(BF16)
    return h1, h2, (r1 - h2.astype(F32)).astype(BF16)


def _sigmoid(x):
    return 0.5 * jnp.tanh(0.5 * x) + 0.5


def _softplus(x):
    return jnp.maximum(x, 0.0) + jnp.log(1.0 + jnp.exp(-jnp.abs(x)))


def _rms_norm_rows(x, g):
    ms = jnp.mean(x * x, axis=-1, keepdims=True)
    return x * lax.rsqrt(ms + RMS_EPS) * g


def _ffn_kernel(h_ref, yc_ref, yr_ref, ys_ref, wo_ref, g_ref, w13_ref, w2_ref, fg_ref, o_ref, act_ref, *,
                d_ff, tf, mix, final):
    hin = h_ref[...]
    if mix:
        r0 = D_CONV
        r1 = D_CONV + D_RWKV
        hin = hin + jnp.dot(yc_ref[...], wo_ref[0:r0, :], preferred_element_type=F32)
        hin = hin + jnp.dot(yr_ref[...], wo_ref[r0:r1, :], preferred_element_type=F32)
        hin = hin + jnp.dot(ys_ref[...], wo_ref[r1:, :], preferred_element_type=F32)
    o_ref[...] = hin
    xn = _rms_norm_rows(hin, g_ref[...]).astype(BF16)
    for c0 in range(0, d_ff, tf):
        gate = jnp.dot(xn, w13_ref[:, c0:c0 + tf], preferred_element_type=F32)
        up = jnp.dot(xn, w13_ref[:, d_ff + c0:d_ff + c0 + tf], preferred_element_type=F32)
        act_ref[:, c0:c0 + tf] = (gate * _sigmoid(gate) * up).astype(BF16)
    out = o_ref[...] + 0.5 * jnp.dot(act_ref[...], w2_ref[...], preferred_element_type=F32)
    if final:
        out = _rms_norm_rows(out, fg_ref[...])
    o_ref[...] = out


def _ffn(h2d, norm_g, w13, w2, layer, *, mix=None, final_g=None, tm=1024, tf=256):
    m, d = h2d.shape
    d_ff = w2.shape[1]
    assert m % tm == 0 and d_ff % tf == 0
    resident = dict(pipeline_mode=pl.Buffered(1))
    rows = lambda w: pl.BlockSpec((tm, w), lambda i: (i, 0))
    if mix is None:
        mix_args = [jnp.zeros((tm, w), BF16) for w in (D_CONV, D_RWKV, D_SB)] + [jnp.zeros((1, 16, d), BF16)]
        mix_specs = [pl.BlockSpec((tm, w), lambda i: (0, 0)) for w in (D_CONV, D_RWKV, D_SB)]
        wo_layer = 0
    else:
        mix_args = list(mix)
        mix_specs = [rows(D_CONV), rows(D_RWKV), rows(D_SB)]
        wo_layer = layer
    mix_specs.append(pl.BlockSpec((None,) + mix_args[3].shape[1:], lambda i: (wo_layer, 0, 0), **resident))
    fg = jnp.zeros((1, d), F32) if final_g is None else final_g
    return pl.pallas_call(
        functools.partial(_ffn_kernel, d_ff=d_ff, tf=tf, mix=mix is not None, final=final_g is not None),
        grid=(m // tm,),
        in_specs=[rows(d)] + mix_specs + [
            pl.BlockSpec((None, 1, d), lambda i: (layer, 0, 0)),
            pl.BlockSpec((None, d, 2 * d_ff), lambda i: (layer, 0, 0), **resident),
            pl.BlockSpec((None, d_ff, d), lambda i: (layer, 0, 0), **resident),
            pl.BlockSpec((1, d), lambda i: (0, 0)),
        ],
        out_specs=rows(d),
        out_shape=jax.ShapeDtypeStruct((m, d), F32),
        scratch_shapes=[pltpu.VMEM((tm, d_ff), BF16)],
        compiler_params=_cparams(1),
        name="ffn",
    )(h2d, *mix_args, norm_g, w13, w2, fg)


def _inproj_kernel(h_ref, g_ref, w_ref, pc_ref, pr_ref, ps_ref):
    xn = _rms_norm_rows(h_ref[...], g_ref[...]).astype(BF16)
    c0 = 2 * D_CONV
    c1 = c0 + D_RWKV_IN
    pc_ref[...] = jnp.dot(xn, w_ref[:, 0:c0], preferred_element_type=F32)
    pr_ref[...] = jnp.dot(xn, w_ref[:, c0:c1], preferred_element_type=F32)
    ps_ref[...] = jnp.dot(xn, w_ref[:, c1:], preferred_element_type=F32).astype(BF16)


def _inproj(h2d, norm_g, w_in, layer, *, tm=1024):
    m, d = h2d.shape
    assert m % tm == 0
    d_in = w_in.shape[2]
    widths = (2 * D_CONV, D_RWKV_IN, 3 * D_SB)
    return pl.pallas_call(
        _inproj_kernel,
        grid=(m // tm,),
        in_specs=[
            pl.BlockSpec((tm, d), lambda i: (i, 0)),
            pl.BlockSpec((None, 1, d), lambda i: (layer, 0, 0)),
            pl.BlockSpec((None, d, d_in), lambda i: (layer, 0, 0), pipeline_mode=pl.Buffered(1)),
        ],
        out_specs=[pl.BlockSpec((tm, w), lambda i: (i, 0)) for w in widths],
        out_shape=[jax.ShapeDtypeStruct((m, w), dt) for w, dt in zip(widths, (F32, F32, BF16))],
        compiler_params=_cparams(1),
        name="inproj",
    )(h2d, norm_g, w_in)


CONV_PAD = 32
CONV_SUB = 64


def _conv_tile(e, p_ref, w_ref, b_ref, lg_ref, lb_ref, o_ref, buf_ref, shift_ref, tt):
    val = p_ref[e, :, 0:D_CONV]
    gate = p_ref[e, :, D_CONV:2 * D_CONV]
    buf_ref[e, CONV_PAD:CONV_PAD + tt, :] = val * _sigmoid(gate)
    first = CONV_PAD - (CONV_WIDTH - 1)
    for b in range(8):
        span = tt + 8 * ((CONV_WIDTH - 1 - b) // 8)
        shift_ref[e, b, 0:span, :] = buf_ref[e, first + b:first + b + span, :]
    for s in range(tt // CONV_SUB):
        acc = jnp.zeros((CONV_SUB, D_CONV), F32)
        for j in range(CONV_WIDTH):
            r0 = s * CONV_SUB + 8 * (j // 8)
            acc = acc + w_ref[j:j + 1, :] * shift_ref[e, j % 8, r0:r0 + CONV_SUB, :]
        y = acc + b_ref[...]
        mu = jnp.mean(y, axis=-1, keepdims=True)
        dlt = y - mu
        var = jnp.mean(dlt * dlt, axis=-1, keepdims=True)
        yn = dlt * lax.rsqrt(var + LN_EPS) * lg_ref[...] + lb_ref[...]
        o_ref[e, s * CONV_SUB:(s + 1) * CONV_SUB, :] = (yn * _sigmoid(yn)).astype(BF16)
    buf_ref[e, 0:CONV_PAD, :] = buf_ref[e, tt:tt + CONV_PAD, :]


def _rwkv_conv_kernel(p_ref, mu_ref, w0_ref, wb_ref, a0_ref, ab_ref, gb_ref, kk_ref, ka_ref, rk_ref, lg_ref, lb_ref,
                      bd_ref, tri_ref, pc_ref, cw_ref, cb_ref, clg_ref, clb_ref, o_ref, oc_ref,
                      st_ref, carry_ref, buf_ref, shift_ref, *, nb, tt):
    c = RWKV_CHUNK
    c2 = 2 * c
    n_pairs = D_RWKV // LANES
    chunks_per_seq = tt // c
    n_chunks = nb * chunks_per_seq
    t = pl.program_id(1)

    @pl.when(t == 0)
    def _():
        st_ref[...] = jnp.zeros_like(st_ref)
        carry_ref[...] = jnp.zeros_like(carry_ref)
        buf_ref[:, 0:CONV_PAD, :] = jnp.zeros((nb, CONV_PAD, D_CONV), F32)


    x = p_ref[...].reshape(nb * tt, D_RWKV_IN)
    row = lax.broadcasted_iota(jnp.int32, x.shape, 0)
    prev = pltpu.roll(x, 1, 0)
    for e in range(nb):
        prev = jnp.where(row == e * tt, carry_ref[e], prev)
        carry_ref[e] = x[(e + 1) * tt - 1:(e + 1) * tt, :]
    xs = x + mu_ref[...] * (prev - x)

    r = xs[:, 0:D_RWKV]
    k = xs[:, D_RWKV:2 * D_RWKV]
    v = xs[:, 2 * D_RWKV:3 * D_RWKV]
    lora_in = xs[:, 3 * D_RWKV:3 * D_RWKV + DECAY_LORA + AAA_LORA]
    gd = xs[:, 3 * D_RWKV + DECAY_LORA + AAA_LORA:]

    bd = bd_ref[...]

    def head_sum(z):
        zb = z.astype(BF16)
        return jnp.concatenate([jnp.dot(zb[:, j * LANES:(j + 1) * LANES], bd, preferred_element_type=F32)
                                for j in range(n_pairs)], axis=1)

    w_log = -_softplus(-(w0_ref[...] + _dot(jnp.tanh(lora_in), wb_ref[...]))) - 0.5
    logw = -jnp.exp(w_log)
    a = _sigmoid(a0_ref[...] + _dot(lora_in, ab_ref[...]))
    g = _dot(_sigmoid(gd), gb_ref[...])
    kk = k * kk_ref[...]
    kk = kk * lax.rsqrt(jnp.maximum(head_sum(kk * kk), 1e-24))
    kq = k * (1.0 + (a - 1.0) * ka_ref[...])
    na = -kk
    b = kk * a
    bonus = head_sum(r * kq * rk_ref[...]) * v

    tri = tri_ref[...]
    l1, l2, l3 = _split3(logw)
    cl = (jnp.dot(tri, l1, preferred_element_type=F32) + jnp.dot(tri, l2, preferred_element_type=F32)
          + jnp.dot(tri, l3, preferred_element_type=F32))
    cl_ends = [cl[(s + 1) * c - 1:(s + 1) * c, :] for s in range(n_chunks)]
    cl_end = jnp.concatenate([jnp.broadcast_to(e, (c, D_RWKV)) for e in cl_ends], axis=0)
    e_pos = jnp.exp(cl)
    e_neg = jnp.exp(-cl)
    e_end = jnp.exp(cl_end - cl)
    at_all = jnp.exp(cl - logw) * na
    bt_all = b * e_neg
    kt_all = kq * e_neg
    rt_all = r * e_pos
    bp_all = b * e_end
    kp_all = kq * e_end

    lane = lax.broadcasted_iota(jnp.int32, (c, LANES), 1)
    lane2 = lax.broadcasted_iota(jnp.int32, (c2, LANES), 1)
    m0 = lane < HEAD_DIM
    m0_2 = lane2 < HEAD_DIM
    ri = lax.broadcasted_iota(jnp.int32, (c2, c2), 0)
    cj = lax.broadcasted_iota(jnp.int32, (c2, c2), 1)
    ti = jnp.bitwise_and(ri, c - 1)
    sj = jnp.bitwise_and(cj, c - 1)
    mask_z = (ti > sj) | ((ri >= c) & (ti == sj))
    left_top = lax.broadcasted_iota(jnp.int32, (c, c2), 1) < c
    blockdiag = jnp.bitwise_and(ri, c) == jnp.bitwise_and(cj, c)
    eye = (ri == cj)
    zero = jnp.zeros((c, LANES), F32)
    rows = lambda parts: jnp.concatenate(parts, axis=0)
    cols = lambda parts: jnp.concatenate(parts, axis=1)

    units = [(s, j) for s in range(n_chunks) for j in range(n_pairs)]

    def tile(arr, u):
        s, j = u
        return arr[s * c:(s + 1) * c, j * LANES:(j + 1) * LANES]

    at = [tile(at_all, u) for u in units]
    rt = [tile(rt_all, u) for u in units]
    vv = [tile(v, u) for u in units]

    z0, z1 = [], []
    for i, u in enumerate(units):
        xx = rows([at[i], rt[i]])
        bt, kt = tile(bt_all, u), tile(kt_all, u)
        yy = rows([jnp.where(m0_2, rows([bt, kt]), 0.0), jnp.where(m0_2, 0.0, rows([kt, bt]))])
        zz = _dot_nt(xx, yy)
        z0.append(jnp.where(mask_z, zz[:, 0:c2], 0.0))
        z1.append(jnp.where(mask_z, zz[:, c2:], 0.0))

    pw = [rows([jnp.where(left_top, z0[i][0:c], 0.0), jnp.where(left_top, 0.0, z1[i][0:c])]) for i in range(len(units))]
    identity = jnp.where(eye, 1.0, 0.0)
    tinv = [identity + p for p in pw]
    for _ in range(5):
        pw = [_dot(p, p) for p in pw]
        tinv = [ti_ + _dot(ti_, p) for ti_, p in zip(tinv, pw)]

    akv = []
    for i in range(len(units)):
        v2 = rows([vv[i], vv[i]])
        akv.append(jnp.where(m0, _dot(jnp.where(left_top, 0.0, z0[i][0:c]), v2),
                             _dot(jnp.where(left_top, z1[i][0:c], 0.0), v2)))

    ah, uv = [], []
    for i in range(len(units)):
        wa = rows([jnp.where(m0, at[i], 0.0), jnp.where(m0, 0.0, at[i])])
        wv = rows([jnp.where(m0, akv[i], 0.0), jnp.where(m0, 0.0, akv[i])])
        tw = _dot(tinv[i], cols([wa, wv]))
        ah.append(tw[0:c, 0:LANES] + tw[c:c2, 0:LANES])
        uv.append(tw[0:c, LANES:] + tw[c:c2, LANES:])

    rh, ov, gmat, hmat = [], [], [], []
    for i, u in enumerate(units):
        rhs0 = cols([rows([ah[i], zero]), rows([uv[i], vv[i]])])
        rhs1 = cols([rows([zero, ah[i]]), rows([vv[i], uv[i]])])
        res0 = _dot(z0[i][c:c2], rhs0)
        res1 = _dot(z1[i][c:c2], rhs1)
        rh.append(rt[i] + jnp.where(m0, res0[:, 0:LANES], res1[:, 0:LANES]))
        ov.append(jnp.where(m0, res0[:, LANES:], res1[:, LANES:]))
        ypt = rows([tile(bp_all, u), tile(kp_all, u)]).T
        gh = _dot(ypt, rhs0)
        s, j = u
        p_end = jnp.exp(cl_ends[s][:, j * LANES:(j + 1) * LANES])
        gmat.append(jnp.where(blockdiag, gh[:, 0:LANES], 0.0)
                    + jnp.where(eye, jnp.broadcast_to(p_end, (c2, LANES)), 0.0))
        hmat.append(jnp.where(blockdiag, gh[:, LANES:], 0.0))

    states = [st_ref[n] for n in range(nb * n_pairs)]
    out_rows = []
    for s in range(n_chunks):
        out_pairs = []
        for j in range(n_pairs):
            i = s * n_pairs + j
            n = (s // chunks_per_seq) * n_pairs + j
            out_pairs.append(_dot(rh[i], states[n]) + ov[i])
            states[n] = _dot(gmat[i], states[n]) + hmat[i]
        out_rows.append(cols(out_pairs))
    for n in range(nb * n_pairs):
        st_ref[n] = states[n]

    o = rows(out_rows)
    mean = head_sum(o) * (1.0 / HEAD_DIM)
    dlt = o - mean
    var = head_sum(dlt * dlt) * (1.0 / HEAD_DIM)
    on = dlt * lax.rsqrt(var + GN_EPS) * lg_ref[...] + lb_ref[...]
    o_ref[...] = ((on + bonus) * g).astype(BF16).reshape(nb, tt, D_RWKV)

    for e in range(nb):
        _conv_tile(e, pc_ref, cw_ref, cb_ref, clg_ref, clb_ref, oc_ref, buf_ref, shift_ref, tt)


def _rwkv_conv_group(pr, pc, rwkv_params, conv_params, bd, layer, *, nb=4, tt=64):
    b, lp, _ = pr.shape
    assert b % nb == 0 and lp % tt == 0 and tt % RWKV_CHUNK == 0 and tt % CONV_SUB == 0
    n_pairs = D_RWKV // LANES
    idx = jnp.arange(nb * tt)
    tri = ((idx[:, None] >= idx[None, :]) & (idx[:, None] // RWKV_CHUNK == idx[None, :] // RWKV_CHUNK)).astype(BF16)

    def par(arr):
        return pl.BlockSpec((None,) + arr.shape[1:], lambda i, t: (layer,) + (0,) * (arr.ndim - 1))

    def const(arr):
        return pl.BlockSpec(arr.shape, lambda i, t: (0,) * arr.ndim)

    tile = lambda w: pl.BlockSpec((nb, tt, w), lambda i, t: (i, t, 0))
    return pl.pallas_call(
        functools.partial(_rwkv_conv_kernel, nb=nb, tt=tt),
        grid=(b // nb, lp // tt),
        in_specs=[tile(D_RWKV_IN)] + [par(p) for p in rwkv_params] + [const(bd), const(tri)]
        + [tile(2 * D_CONV)] + [par(p) for p in conv_params],
        out_specs=[tile(D_RWKV), tile(D_CONV)],
        out_shape=[jax.ShapeDtypeStruct((b, lp, D_RWKV), BF16), jax.ShapeDtypeStruct((b, lp, D_CONV), BF16)],
        scratch_shapes=[pltpu.VMEM((nb * n_pairs, LANES, LANES), F32), pltpu.VMEM((nb, 1, D_RWKV_IN), F32),
                        pltpu.VMEM((nb, tt + CONV_PAD, D_CONV), F32),
                        pltpu.VMEM((nb, 8, tt + 8 * ((CONV_WIDTH - 1) // 8), D_CONV), F32)],
        compiler_params=_cparams(2),
        name="rwkv_conv_group",
    )(pr, *rwkv_params, bd, tri, pc, *conv_params)


SB_TILE = 128
SB_GROUP_TILES = 8


def _sb_kernel(q_ref, k_ref, v_ref, g_ref, bd_ref, later_ref, o_ref, km_ref, vm_ref, *, nb):
    t = SB_TILE
    n_pairs = D_SB // LANES
    n_streams = nb * n_pairs
    qi = pl.program_id(1)

    @pl.when(qi == 0)
    def _():
        m0 = lax.broadcasted_iota(jnp.int32, (k_ref.shape[1], LANES), 1) < HEAD_DIM
        for p in range(n_streams):
            e, j = divmod(p, n_pairs)
            kj = k_ref[e, :, j * LANES:(j + 1) * LANES]
            vj = v_ref[e, :, j * LANES:(j + 1) * LANES]
            zero = jnp.zeros_like(kj)
            km_ref[2 * p] = jnp.where(m0, kj, zero)
            km_ref[2 * p + 1] = jnp.where(m0, zero, kj)
            vm_ref[2 * p] = jnp.where(m0, vj, zero)
            vm_ref[2 * p + 1] = jnp.where(m0, zero, vj)

    rowi = lax.broadcasted_iota(jnp.int32, (t, t), 0)
    coli = lax.broadcasted_iota(jnp.int32, (t, t), 1)
    causal = coli < rowi
    causal2 = jnp.concatenate([causal, causal], axis=1)
    later2 = later_ref[...]
    qb = [q_ref[p // n_pairs, :, (p % n_pairs) * LANES:(p % n_pairs + 1) * LANES] * (HEAD_DIM ** -0.5)
          for p in range(n_streams)]

    def both_heads(ref, kb, p):
        off = kb * t if isinstance(kb, int) else pl.multiple_of(kb * t, t)
        return jnp.concatenate([ref[2 * p, pl.ds(off, t), :], ref[2 * p + 1, pl.ds(off, t), :]], axis=0)

    def scores(kbs):
        return tuple(lax.dot_general(qb[p], both_heads(km_ref, kb, p), (((1,), (1,)), ((), ())),
                                     preferred_element_type=F32) for kb in kbs for p in range(n_streams))

    def group(kbs, zs, accs, carries, diagonal):
        units = [(kb, p) for kb in kbs for p in range(n_streams)]
        sps = [_softplus(z.astype(BF16)) for z in zs]
        if diagonal:
            sps = [jnp.where(causal2, sp, jnp.zeros_like(sp)) for sp in sps]
        locs = [jnp.dot(sp, later2, preferred_element_type=F32) for sp in sps]
        accs, carries = list(accs), list(carries)
        for i, (kb, p) in enumerate(units):
            halves = []
            for m in range(2):
                sl = slice(m * t, (m + 1) * t)
                att = jnp.exp(zs[i][:, sl] - locs[i][:, sl] - carries[2 * p + m])
                if diagonal:
                    att = jnp.where(causal, att, 0.0)
                halves.append(att.astype(BF16))
                carries[2 * p + m] = carries[2 * p + m] + locs[i][:, m * t:m * t + 1]
            accs[p] = accs[p] + jnp.dot(jnp.concatenate(halves, axis=1), both_heads(vm_ref, kb, p),
                                        preferred_element_type=F32)
        return tuple(accs), tuple(carries)

    state = (tuple(jnp.zeros((t, LANES), F32) for _ in range(n_streams)),
             tuple(jnp.zeros((t, 1), F32) for _ in range(2 * n_streams)))
    state = group([qi], scores([qi]), *state, True)

    def run(first, n_iters, width, state):
        def body(it, c):
            kbs = [first - width * it - w for w in range(width)]
            zs_next = scores(kbs[0:1])
            for w in range(width):
                zs = zs_next
                if w + 1 < width:
                    zs_next = scores(kbs[w + 1:w + 2])
                c = group(kbs[w:w + 1], zs, *c, False)
            return c
        return lax.fori_loop(0, n_iters, body, state)

    width = max(1, SB_GROUP_TILES // nb)
    n = jnp.right_shift(qi, width.bit_length() - 1)
    state = run(qi - 1, n, width, state)
    done = n * width
    while width > 1:
        width //= 2
        n = jnp.bitwise_and(jnp.right_shift(qi, width.bit_length() - 1), 1)
        state = run(qi - 1 - done, n, width, state)
        done = done + n * width
    accs = state[0]

    for p in range(n_streams):
        e, j = divmod(p, n_pairs)
        o = accs[p]
        ss = jnp.dot((o * o).astype(BF16), bd_ref[...], preferred_element_type=F32)
        o_ref[e, :, j * LANES:(j + 1) * LANES] = (o * lax.rsqrt(ss * (1.0 / HEAD_DIM) + RMS_EPS)
                                                  * g_ref[:, j * LANES:(j + 1) * LANES]).astype(BF16)


def _sb_group(ps, norm_g, bd128, layer, *, nb=2):
    b, lp, _ = ps.shape
    assert lp % SB_TILE == 0 and b % nb == 0
    n_pairs = D_SB // LANES
    idx = jnp.arange(2 * SB_TILE)
    later2 = ((idx[:, None] >= idx[None, :]) & (idx[:, None] // SB_TILE == idx[None, :] // SB_TILE)).astype(BF16)
    masked_copies = pltpu.VMEM((2 * nb * n_pairs, lp, LANES), BF16)
    return pl.pallas_call(
        functools.partial(_sb_kernel, nb=nb),
        grid=(b // nb, lp // SB_TILE),
        in_specs=[
            pl.BlockSpec((nb, SB_TILE, D_SB), lambda i, q: (i, q, 0)),
            pl.BlockSpec((nb, lp, D_SB), lambda i, q: (i, 0, 1)),
            pl.BlockSpec((nb, lp, D_SB), lambda i, q: (i, 0, 2)),
            pl.BlockSpec((None, 1, D_SB), lambda i, q: (layer, 0, 0)),
            pl.BlockSpec((LANES, LANES), lambda i, q: (0, 0)),
            pl.BlockSpec(later2.shape, lambda i, q: (0, 0)),
        ],
        out_specs=pl.BlockSpec((nb, SB_TILE, D_SB), lambda i, q: (i, q, 0)),
        out_shape=jax.ShapeDtypeStruct((b, lp, D_SB), BF16),
        scratch_shapes=[masked_copies, masked_copies],
        compiler_params=_cparams(2),
        name="sb_group",
    )(ps, ps, ps, norm_g, bd128, later2)


def _block_diag_ones(n):
    i = jnp.arange(n) // HEAD_DIM
    return (i[:, None] == i[None, :]).astype(BF16)


@jax.jit
def _trunk(x, meta, ffn1_norm, ffn1_w13, ffn1_w2, mix_norm, w_in, conv_w, conv_b, conv_ln_g, conv_ln_b, rwkv_mu,
           rwkv_w0, rwkv_wB, rwkv_a0, rwkv_aB, rwkv_gB, rwkv_kk, rwkv_ka, rwkv_rk, rwkv_ln_g, rwkv_ln_b, sb_norm,
           w_out, ffn2_norm, ffn2_w13, ffn2_w2, final_norm):
    bsz, seq, d = x.shape
    depth = w_in.shape[0]
    l_real = N_META + seq
    lp = -(-l_real // SEQ_ALIGN) * SEQ_ALIGN
    meta_b = jnp.broadcast_to(meta.astype(x.dtype)[None], (bsz, N_META, d))
    h = jnp.concatenate([meta_b, x, jnp.zeros((bsz, lp - l_real, d), x.dtype)], axis=1).reshape(bsz * lp, d)

    row3 = lambda p: p.reshape(depth, 1, -1)
    ffn1_w13, ffn1_w2, ffn2_w13, ffn2_w2 = (w.astype(BF16) for w in (ffn1_w13, ffn1_w2, ffn2_w13, ffn2_w2))
    w_in, w_out = w_in.astype(BF16), w_out.astype(BF16)
    ffn1_norm, mix_norm, ffn2_norm, sb_norm = row3(ffn1_norm), row3(mix_norm), row3(ffn2_norm), row3(sb_norm)
    conv_b, conv_ln_g, conv_ln_b = row3(conv_b), row3(conv_ln_g), row3(conv_ln_b)
    rwkv_vecs = [row3(p) for p in (rwkv_mu, rwkv_w0, rwkv_a0, rwkv_kk, rwkv_ka, rwkv_rk, rwkv_ln_g, rwkv_ln_b)]
    mu, w0, a0, k_k, k_a, r_k, ln_g, ln_b = rwkv_vecs
    wb_ext = jnp.pad(rwkv_wB, ((0, 0), (0, AAA_LORA), (0, 0))).astype(BF16)
    ab_ext = jnp.pad(rwkv_aB, ((0, 0), (DECAY_LORA, 0), (0, 0))).astype(BF16)
    gb = rwkv_gB.astype(BF16)
    bd_pair = _block_diag_ones(LANES)

    for l in range(depth):
        h = _ffn(h, ffn1_norm, ffn1_w13, ffn1_w2, l)
        pc, pr, ps = _inproj(h, mix_norm, w_in, l)
        yr, yc = _rwkv_conv_group(pr.reshape(bsz, lp, -1), pc.reshape(bsz, lp, -1),
                                  (mu, w0, wb_ext, a0, ab_ext, gb, k_k, k_a, r_k, ln_g, ln_b),
                                  (conv_w, conv_b, conv_ln_g, conv_ln_b), bd_pair, l)
        ys = _sb_group(ps.reshape(bsz, lp, -1), sb_norm, bd_pair, l)
        mix = (yc.reshape(bsz * lp, -1), yr.reshape(bsz * lp, -1), ys.reshape(bsz * lp, -1), w_out)
        h = _ffn(h, ffn2_norm, ffn2_w13, ffn2_w2, l, mix=mix,
                 final_g=final_norm.reshape(1, d) if l == depth - 1 else None)
    out = h
    return out.reshape(bsz, lp, d)[:, N_META:l_real]


def kernel(x, meta, ffn1_norm, ffn1_w13, ffn1_w2, mix_norm, w_in, conv_w, conv_b, conv_ln_g, conv_ln_b, rwkv_mu,
           rwkv_w0, rwkv_wB, rwkv_a0, rwkv_aB, rwkv_gB, rwkv_kk, rwkv_ka, rwkv_rk, rwkv_ln_g, rwkv_ln_b, sb_norm,
           w_out, ffn2_norm, ffn2_w13, ffn2_w2, final_norm):
    return _trunk(x, meta, ffn1_norm, ffn1_w13, ffn1_w2, mix_norm, w_in, conv_w, conv_b, conv_ln_g, conv_ln_b,
                  rwkv_mu, rwkv_w0, rwkv_wB, rwkv_a0, rwkv_aB, rwkv_gB, rwkv_kk, rwkv_ka, rwkv_rk, rwkv_ln_g,
                  rwkv_ln_b, sb_norm, w_out, ffn2_norm, ffn2_w13, ffn2_w2, final_norm)
```
